```python
import jax, jax.numpy as jnp
from jax import lax
import numpy as np

D_MODEL = 1024
BATCH = 8
SEQ = 2048
DEPTH = 4
DEC_BATCH = 128
DEC_SEQ = 8
PAST_LEN = 16384
PAGE_SIZE = 128

GLA_HEADS = 4
GLA_DK = D_MODEL // (2 * GLA_HEADS)
GLA_DV = D_MODEL // GLA_HEADS
GLA_RANK = 16
GLA_GATE_TEMP = 16.0
GLA_CHUNK = 64
ML_HEADS = 4
ML_DK = D_MODEL // (2 * ML_HEADS)
ML_DV = D_MODEL // ML_HEADS
ML_CHUNK = 64
RG_WIDTH = D_MODEL
RG_BLOCKS = 8
RG_BW = RG_WIDTH // RG_BLOCKS
RG_C = 8.0
CONV_W = 4
N_BRANCH = 3
PEER_HEADS = 8
PEER_NKEYS = 128
PEER_EXPERTS = PEER_NKEYS * PEER_NKEYS
PEER_DK = 256
PEER_TOPK = 16
PEER_TOKEN_BLOCK = 256
EPS = 1e-6
IN_SIZES = (GLA_HEADS * GLA_DK, GLA_HEADS * GLA_DK, GLA_HEADS * GLA_DV, GLA_HEADS * GLA_DV, GLA_RANK,
            ML_HEADS * ML_DK, ML_HEADS * ML_DK, ML_HEADS * ML_DV, ML_HEADS * ML_DV, ML_HEADS, ML_HEADS,
            RG_WIDTH, N_BRANCH * D_MODEL)
IN_COLS = sum(IN_SIZES)

kernel_name = 'hybrid_gla_mlstm_rglru_peer_decode_step'


def rmsnorm(x, g):
    x32 = x.astype(jnp.float32)
    y = x32 * lax.rsqrt(jnp.mean(x32 * x32, axis=-1, keepdims=True) + EPS)
    return (y * g.astype(jnp.float32)).astype(x.dtype)


def chunk_len(T, C):
    return C if T % C == 0 else T


def to_chunks(a, n, c):
    return jnp.moveaxis(a.reshape(a.shape[0], n, c, *a.shape[2:]), 1, 0)


def gla_mixer(q, k, v, logf, s0):
    B, T, H, DK = q.shape
    DV = v.shape[-1]
    C = chunk_len(T, GLA_CHUNK)
    N = T // C
    causal = jnp.tril(jnp.ones((C, C), dtype=bool))

    def step(S, xs):
        qc, kc, vc, fc = xs
        b = jnp.cumsum(fc, axis=1)
        b_last = b[:, -1]
        q_dec = qc * jnp.exp(b)
        k_dec = kc * jnp.exp(-b)
        att = jnp.where(causal, jnp.einsum('bthd,bshd->bhts', q_dec, k_dec), 0.0)
        o = jnp.einsum('bhts,bshv->bthv', att, vc) + jnp.einsum('bthd,bhdv->bthv', q_dec, S)
        k_end = kc * jnp.exp(b_last[:, None] - b)
        S_new = jnp.exp(b_last)[..., None] * S + jnp.einsum('bshd,bshv->bhdv', k_end, vc)
        return S_new, o

    s_fin, o = lax.scan(step, s0, (to_chunks(q, N, C), to_chunks(k, N, C), to_chunks(v, N, C), to_chunks(logf, N, C)))
    return jnp.moveaxis(o, 0, 1).reshape(B, T, H, DV), s_fin


def mlstm_mixer(q, k, v, ig, logf, c0, n0, m0):
    B, T, H, DK = q.shape
    DV = v.shape[-1]
    C = chunk_len(T, ML_CHUNK)
    N = T // C
    causal = jnp.tril(jnp.ones((C, C), dtype=bool))[None, :, :, None]

    def step(carry, xs):
        Cs, ns, ms = carry
        qc, kc, vc, ic, fc = xs
        F = jnp.cumsum(fc, axis=1)
        m = F + jnp.maximum(ms[:, None], lax.cummax(ic - F, axis=1))
        w_inter = jnp.exp(F + ms[:, None] - m)
        log_d = (F - m)[:, :, None, :] + (ic - F)[:, None, :, :]
        d = jnp.exp(jnp.where(causal, log_d, -jnp.inf))
        qk = jnp.einsum('bthd,bshd->btsh', qc, kc) * d
        num = jnp.einsum('btsh,bshv->bthv', qk, vc) + w_inter[..., None] * jnp.einsum('bthd,bhdv->bthv', qc, Cs)
        den = jnp.sum(qk, axis=2) + w_inter * jnp.einsum('bthd,bhd->bth', qc, ns)
        h = num / jnp.maximum(jnp.abs(den), jnp.exp(-m))[..., None]
        m_last = m[:, -1]
        F_last = F[:, -1]
        w_state = jnp.exp(ic + F_last[:, None] - F - m_last[:, None])
        carry_scale = jnp.exp(F_last + ms - m_last)
        C_new = carry_scale[..., None, None] * Cs + jnp.einsum('bsh,bshd,bshv->bhdv', w_state, kc, vc)
        n_new = carry_scale[..., None] * ns + jnp.einsum('bsh,bshd->bhd', w_state, kc)
        return (C_new, n_new, m_last), h

    xs = (to_chunks(q, N, C), to_chunks(k, N, C), to_chunks(v, N, C), to_chunks(ig, N, C), to_chunks(logf, N, C))
    (c_fin, n_fin, m_fin), h = lax.scan(step, (c0, n0, m0), xs)
    return jnp.moveaxis(h, 0, 1).reshape(B, T, H, DV), c_fin, n_fin, m_fin


def rglru_mixer(xb, conv_buf, h0, conv_w, conv_b, w_a, b_a, w_x, b_x, lam):
    f32 = jnp.float32
    B, T, W = xb.shape
    xpad = jnp.concatenate([conv_buf.astype(f32), xb.astype(f32)], axis=1)
    xc = conv_b.astype(f32)
    for j in range(CONV_W):
        xc = xc + xpad[:, j:j + T] * conv_w[j].astype(f32)
    new_buf = xpad[:, T:]
    xg = xc.reshape(B, T, RG_BLOCKS, RG_BW)
    r = jax.nn.sigmoid(jnp.einsum('btnw,nwv->btnv', xg, w_a.astype(f32)).reshape(B, T, W) + b_a)
    i = jax.nn.sigmoid(jnp.einsum('btnw,nwv->btnv', xg, w_x.astype(f32)).reshape(B, T, W) + b_x)
    log_a = -RG_C * r * jax.nn.softplus(-lam.astype(f32))
    a = jnp.exp(log_a)
    u = jnp.sqrt(-jnp.expm1(2.0 * log_a)) * (i * xc)

    def step(h, xs):
        a_t, u_t = xs
        h = a_t * h + u_t
        return h, h

    h_fin, hs = lax.scan(step, h0.astype(f32), (jnp.swapaxes(a, 0, 1), jnp.swapaxes(u, 0, 1)))
    return jnp.swapaxes(hs, 0, 1), h_fin, new_buf


def peer_ffn(h, w_q, keys, u_tab, v_tab):
    f32 = jnp.float32
    B, T, D = h.shape
    n_tok = B * T
    blk = min(PEER_TOKEN_BLOCK, n_tok)
    n_blk = -(-n_tok // blk)
    pad = n_blk * blk - n_tok
    hf = jnp.pad(h.reshape(n_tok, D), ((0, pad), (0, 0))).reshape(n_blk, blk, D)
    K = PEER_TOPK

    def one_block(hb):
        q = jnp.dot(hb, w_q).astype(f32).reshape(blk, PEER_HEADS, 2, PEER_DK // 2)
        s = jnp.einsum('thpd,hpkd->thpk', q, keys.astype(f32))
        sv, si = lax.top_k(s, K)
        cand = (sv[:, :, 0, :, None] + sv[:, :, 1, None, :]).reshape(blk, PEER_HEADS, K * K)
        cidx = (si[:, :, 0, :, None] * PEER_NKEYS + si[:, :, 1, None, :]).reshape(blk, PEER_HEADS, K * K)
        fv, fi = lax.top_k(cand, K)
        eidx = jnp.take_along_axis(cidx, fi, axis=-1)
        g = jax.nn.softmax(fv, axis=-1)
        u = u_tab[eidx]
        act = jax.nn.gelu(jnp.einsum('thkd,td->thk', u, hb).astype(f32), approximate=False)
        return jnp.einsum('thk,thkd->td', (g * act).astype(hb.dtype), v_tab[eidx])

    out = lax.map(one_block, hf)
    return out.reshape(n_blk * blk, D)[:n_tok].reshape(B, T, D)


def decoder_layer(x, c, state, p):
    s_gla, c_ml, n_ml, m_ml, h_rg, buf_rg = state
    B, T, D = x.shape
    f32 = jnp.float32
    mod = jnp.dot(jax.nn.silu(c), p['w_ada']) + p['b_ada']
    sh1, sc1, gt1, sh2, sc2, gt2 = jnp.split(mod[:, None, :], 6, axis=-1)
    h = rmsnorm(x, p['norm1_g']) * (1.0 + sc1) + sh1
    proj = jnp.dot(h, p['w_in'])
    splits = [int(s) for s in np.cumsum(IN_SIZES)[:-1]]
    gq, gk, gv, gg, gf, mq, mk, mv, mo, mi, mf, rx, mg = jnp.split(proj, splits, axis=-1)
    q = gq.astype(f32).reshape(B, T, GLA_HEADS, GLA_DK) * GLA_DK ** -0.5
    k = gk.astype(f32).reshape(B, T, GLA_HEADS, GLA_DK)
    v = gv.astype(f32).reshape(B, T, GLA_HEADS, GLA_DV)
    logf = jax.nn.log_sigmoid((jnp.dot(gf, p['w_gla_f2']) + p['b_gla_f']).astype(f32)).reshape(B, T, GLA_HEADS, GLA_DK) / GLA_GATE_TEMP
    o_gla, s_gla_new = gla_mixer(q, k, v, logf, s_gla.astype(f32))
    o_gla = rmsnorm(o_gla, p['gla_norm_g']) * jax.nn.silu(gg.astype(f32).reshape(B, T, GLA_HEADS, GLA_DV))
    q = mq.astype(f32).reshape(B, T, ML_HEADS, ML_DK)
    k = mk.astype(f32).reshape(B, T, ML_HEADS, ML_DK) * ML_DK ** -0.5
    v = mv.astype(f32).reshape(B, T, ML_HEADS, ML_DV)
    ig = (mi + p['b_ml_i']).astype(f32)
    lf = jax.nn.log_sigmoid((mf + p['b_ml_f']).astype(f32))
    o_ml, c_ml_new, n_ml_new, m_ml_new = mlstm_mixer(q, k, v, ig, lf, c_ml.astype(f32), n_ml.astype(f32), m_ml.astype(f32))
    o_ml = rmsnorm(o_ml, p['ml_norm_g']) * jax.nn.sigmoid(mo.astype(f32).reshape(B, T, ML_HEADS, ML_DV))
    o_rg, h_rg_new, buf_rg_new = rglru_mixer(rx, buf_rg, h_rg, p['conv_w'], p['conv_b'], p['w_rg_a'], p['b_rg_a'], p['w_rg_x'], p['b_rg_x'], p['rg_lambda'])
    gates = jax.nn.sigmoid(mg.reshape(B, T, N_BRANCH, D))
    merged = (gates[:, :, 0] * jnp.dot(o_gla.reshape(B, T, -1).astype(x.dtype), p['w_br_gla'])
              + gates[:, :, 1] * jnp.dot(o_ml.reshape(B, T, -1).astype(x.dtype), p['w_br_ml'])
              + gates[:, :, 2] * jnp.dot(o_rg.astype(x.dtype), p['w_br_rg']))
    x = x + gt1 * jnp.dot(merged, p['w_out'])
    h2 = rmsnorm(x, p['norm2_g']) * (1.0 + sc2) + sh2
    x = x + gt2 * peer_ffn(h2, p['w_peer_q'], p['peer_keys'], p['peer_u'], p['peer_v'])
    return x, (s_gla_new, c_ml_new, n_ml_new, m_ml_new, h_rg_new, buf_rg_new)


def setup_inputs(seed: int = 0) -> dict:
    key = jax.random.key(seed)
    ks = jax.random.split(key, 40)
    f32 = jnp.float32
    L, D = DEPTH, D_MODEL

    def nrm(k, shape, scale):
        return jax.random.normal(k, shape, f32) * scale

    lam_u = jax.random.uniform(ks[0], (L, RG_WIDTH), f32, 0.9, 0.999)
    a_base = lam_u ** (1.0 / RG_C)
    rg_lambda = jnp.log(a_base) - jnp.log1p(-a_base)
    b_ml_f = jnp.linspace(3.0, 6.0, ML_HEADS, dtype=f32)[None, :] + nrm(ks[1], (L, ML_HEADS), 0.1)
    return {
        'x_prompt': nrm(ks[2], (BATCH, SEQ, D), 1.0),
        'x_sample': nrm(ks[3], (DEC_BATCH, DEC_SEQ, D), 1.0),
        'state_gla': nrm(ks[4], (L, DEC_BATCH, GLA_HEADS, GLA_DK, GLA_DV), 1.0),
        'state_mlstm_c': nrm(ks[5], (L, DEC_BATCH, ML_HEADS, ML_DK, ML_DV), 0.5),
        'state_mlstm_n': nrm(ks[6], (L, DEC_BATCH, ML_HEADS, ML_DK), 0.5),
        'state_mlstm_m': nrm(ks[7], (L, DEC_BATCH, ML_HEADS), 1.0),
        'state_rglru_h': nrm(ks[8], (L, DEC_BATCH, RG_WIDTH), 0.5),
        'state_rglru_conv': nrm(ks[9], (L, DEC_BATCH, CONV_W - 1, RG_WIDTH), 1.0),
        'c_prompt': nrm(ks[10], (BATCH, D), 1.0),
        'c_sample': nrm(ks[11], (DEC_BATCH, D), 1.0),
        'w_ada': nrm(ks[12], (L, D, 6 * D), 0.5 * D ** -0.5),
        'b_ada': nrm(ks[13], (L, 6 * D), 0.02),
        'norm1_g': 1.0 + nrm(ks[14], (L, D), 0.02),
        'norm2_g': 1.0 + nrm(ks[15], (L, D), 0.02),
        'w_in': nrm(ks[16], (L, D, IN_COLS), D ** -0.5),
        'w_gla_f2': nrm(ks[17], (L, GLA_RANK, GLA_HEADS * GLA_DK), GLA_RANK ** -0.5),
        'b_gla_f': nrm(ks[18], (L, GLA_HEADS * GLA_DK), 0.1),
        'gla_norm_g': 1.0 + nrm(ks[19], (L, GLA_DV), 0.02),
        'b_ml_i': nrm(ks[20], (L, ML_HEADS), 0.1),
        'b_ml_f': b_ml_f,
        'ml_norm_g': 1.0 + nrm(ks[21], (L, ML_DV), 0.02),
        'conv_w': nrm(ks[22], (L, CONV_W, RG_WIDTH), CONV_W ** -0.5),
        'conv_b': nrm(ks[23], (L, RG_WIDTH), 0.02),
        'w_rg_a': nrm(ks[24], (L, RG_BLOCKS, RG_BW, RG_BW), RG_BW ** -0.5),
        'b_rg_a': nrm(ks[25], (L, RG_WIDTH), 0.1),
        'w_rg_x': nrm(ks[26], (L, RG_BLOCKS, RG_BW, RG_BW), RG_BW ** -0.5),
        'b_rg_x': nrm(ks[27], (L, RG_WIDTH), 0.1),
        'rg_lambda': rg_lambda,
        'w_br_gla': nrm(ks[28], (L, GLA_HEADS * GLA_DV, D), (GLA_HEADS * GLA_DV) ** -0.5),
        'w_br_ml': nrm(ks[29], (L, ML_HEADS * ML_DV, D), (ML_HEADS * ML_DV) ** -0.5),
        'w_br_rg': nrm(ks[30], (L, RG_WIDTH, D), RG_WIDTH ** -0.5),
        'w_out': nrm(ks[31], (L, D, D), D ** -0.5),
        'w_peer_q': nrm(ks[32], (L, D, PEER_HEADS * PEER_DK), D ** -0.5),
        'peer_keys': nrm(ks[33], (L, PEER_HEADS, 2, PEER_NKEYS, PEER_DK // 2), (PEER_DK // 2) ** -0.5),
        'peer_u': nrm(ks[34], (L, PEER_EXPERTS, D), D ** -0.5),
        'peer_v': nrm(ks[35], (L, PEER_EXPERTS, D), PEER_HEADS ** -0.5),
        'final_norm_g': 1.0 + nrm(ks[36], (D,), 0.02),
    }


def reference(x_prompt, x_sample, state_gla, state_mlstm_c, state_mlstm_n, state_mlstm_m, state_rglru_h, state_rglru_conv,
              c_prompt, c_sample, w_ada, b_ada, norm1_g, norm2_g, w_in, w_gla_f2, b_gla_f, gla_norm_g, b_ml_i, b_ml_f,
              ml_norm_g, conv_w, conv_b, w_rg_a, b_rg_a, w_rg_x, b_rg_x, rg_lambda, w_br_gla, w_br_ml, w_br_rg, w_out,
              w_peer_q, peer_keys, peer_u, peer_v, final_norm_g):
    f32 = jnp.float32
    bp = x_prompt.shape[0]
    zero_state = (jnp.zeros((bp, GLA_HEADS, GLA_DK, GLA_DV), f32),
                  jnp.zeros((bp, ML_HEADS, ML_DK, ML_DV), f32),
                  jnp.zeros((bp, ML_HEADS, ML_DK), f32),
                  jnp.zeros((bp, ML_HEADS), f32),
                  jnp.zeros((bp, RG_WIDTH), f32),
                  jnp.zeros((bp, CONV_W - 1, RG_WIDTH), f32))
    yp, ys = x_prompt, x_sample
    new_p, new_s = [], []
    for l in range(DEPTH):
        p = {'w_ada': w_ada[l], 'b_ada': b_ada[l], 'norm1_g': norm1_g[l], 'norm2_g': norm2_g[l], 'w_in': w_in[l],
             'w_gla_f2': w_gla_f2[l], 'b_gla_f': b_gla_f[l], 'gla_norm_g': gla_norm_g[l], 'b_ml_i': b_ml_i[l],
             'b_ml_f': b_ml_f[l], 'ml_norm_g': ml_norm_g[l], 'conv_w': conv_w[l], 'conv_b': conv_b[l],
             'w_rg_a': w_rg_a[l], 'b_rg_a': b_rg_a[l], 'w_rg_x': w_rg_x[l], 'b_rg_x': b_rg_x[l],
             'rg_lambda': rg_lambda[l], 'w_br_gla': w_br_gla[l], 'w_br_ml': w_br_ml[l], 'w_br_rg': w_br_rg[l],
             'w_out': w_out[l], 'w_peer_q': w_peer_q[l], 'peer_keys': peer_keys[l], 'peer_u': peer_u[l],
             'peer_v': peer_v[l]}
        yp, st_p = decoder_layer(yp, c_prompt, zero_state, p)
        ys, st_s = decoder_layer(ys, c_sample, (state_gla[l], state_mlstm_c[l], state_mlstm_n[l], state_mlstm_m[l],
                                                state_rglru_h[l], state_rglru_conv[l]), p)
        new_p.append(st_p)
        new_s.append(st_s)
    yp = rmsnorm(yp, final_norm_g)
    ys = rmsnorm(ys, final_norm_g)
    gla_p, mlc_p, mln_p, mlm_p, rgh_p, rgc_p = [jnp.stack(z) for z in zip(*new_p)]
    gla_s, mlc_s, mln_s, mlm_s, rgh_s, rgc_s = [jnp.stack(z) for z in zip(*new_s)]
    return (yp, ys, gla_p, mlc_p, mln_p, mlm_p, rgh_p, rgc_p, gla_s, mlc_s, mln_s, mlm_s, rgh_s, rgc_s)
```

```python
import functools

import jax
import jax.numpy as jnp
from jax import lax
from jax.experimental import pallas as pl
from jax.experimental.pallas import tpu as pltpu

F32 = jnp.float32
BF16 = jnp.bfloat16
HIGHEST = lax.Precision.HIGHEST

EPS = 1e-6
N_HEADS = 4
DK = 128
DV = 256
GLA_RANK = 16
GLA_GATE_TEMP = 16.0
CHUNK = 64
RG_BLOCKS = 8
RG_C = 8.0
CONV_W = 4
PEER_HEADS = 8
PEER_NKEYS = 128
PEER_TOPK = 16
NEG = -1e30

VMEM_LIMIT_BYTES = 56 * 1024 * 1024

IN_PIECES = (("gq", 512), ("gk", 512), ("gv", 1024), ("gg", 1024),
             ("mq", 512), ("mk", 512), ("mv", 1024), ("mo", 1024),
             ("rx", 1024), ("mg", 3072))
SMALL_W = 128
SM_I = GLA_RANK
SM_F = GLA_RANK + N_HEADS


def _params(sem):
    return pltpu.CompilerParams(dimension_semantics=sem, vmem_limit_bytes=VMEM_LIMIT_BYTES)


def _const_spec(shape, index_map):
    return pl.BlockSpec(shape, index_map, pipeline_mode=pl.Buffered(1))


def _log_sigmoid(x):
    return jnp.minimum(x, 0.0) - jnp.log1p(jnp.exp(-jnp.abs(x)))


def _softplus(x):
    return jnp.maximum(x, 0.0) + jnp.log1p(jnp.exp(-jnp.abs(x)))


def _sigmoid(x):
    return 1.0 / (1.0 + jnp.exp(-x))


def _rms(x, g):
    return x * lax.rsqrt(jnp.mean(x * x, axis=-1, keepdims=True) + EPS) * g


def _pad_rows(a, rows, value=0.0):
    if a.shape[0] == rows:
        return a
    return jnp.concatenate([a, jnp.full((rows - a.shape[0], a.shape[1]), value, a.dtype)], axis=0)


def _row_to_col(row, eye):
    n = row.shape[1]
    return jnp.sum(jnp.where(eye, jnp.broadcast_to(row, (n, n)), 0.0), axis=1, keepdims=True)


def _col_to_row(col, eye):
    n = col.shape[0]
    return jnp.sum(jnp.where(eye, jnp.broadcast_to(col, (n, n)), 0.0), axis=0, keepdims=True)


def _mod_kernel(c_ref, w_ref, b_ref, o_ref):
    c = c_ref[...]
    s = (c * _sigmoid(c)).astype(BF16)
    o_ref[0] = jnp.dot(s, w_ref[0].astype(BF16), preferred_element_type=F32) + b_ref[0]


def _modulation(c_all, w_ada, b_ada):
    n_layers, d, n6 = w_ada.shape
    rows = c_all.shape[0]
    tn = 1536
    return pl.pallas_call(
        _mod_kernel,
        grid=(n_layers, n6 // tn),
        in_specs=[pl.BlockSpec((rows, d), lambda l, n: (0, 0)),
                  pl.BlockSpec((1, d, tn), lambda l, n: (l, 0, n)),
                  pl.BlockSpec((1, 1, tn), lambda l, n: (l, 0, n))],
        out_specs=pl.BlockSpec((1, rows, tn), lambda l, n: (l, 0, n)),
        out_shape=jax.ShapeDtypeStruct((n_layers, rows, n6), F32),
        compiler_params=_params(("arbitrary", "arbitrary")),
        name="adaln_mod",
    )(c_all, w_ada, b_ada.reshape(n_layers, 1, n6))


def _inproj_kernel(x_ref, mod_ref, g_ref, wm_ref, ws_ref, *out_refs):
    bb, tb, d = x_ref.shape
    mod = mod_ref[...]
    h = _rms(x_ref[...], g_ref[...]) * (1.0 + mod[:, :, d:2 * d]) + mod[:, :, 0:d]
    hb = h.reshape(bb * tb, d).astype(BF16)
    off = 0
    for r, (_, w) in zip(out_refs[:-1], IN_PIECES):
        r[...] = jnp.dot(hb, wm_ref[0, :, off:off + w], preferred_element_type=F32).astype(r.dtype)
        off += w
    out_refs[-1][...] = jnp.dot(hb, ws_ref[0], preferred_element_type=F32)


def _in_projection(x, mod, g1, w_main, w_small, layer, bb, tb, out_dtype, rx_time_major):
    bn, t, d = x.shape
    nt = t // tb
    rows = bb * tb
    n_main = w_main.shape[-1]
    out_shapes, out_specs = [], []
    for name, w in IN_PIECES:
        if name == "rx" and rx_time_major:
            out_shapes.append(jax.ShapeDtypeStruct((t, bn * w), out_dtype))
            out_specs.append(pl.BlockSpec((tb, w), lambda i, j: (j, i)))
        else:
            out_shapes.append(jax.ShapeDtypeStruct((bn * t, w), out_dtype))
            out_specs.append(pl.BlockSpec((rows, w), lambda i, j: (i * nt + j, 0)))
    out_shapes.append(jax.ShapeDtypeStruct((bn * t, SMALL_W), F32))
    out_specs.append(pl.BlockSpec((rows, SMALL_W), lambda i, j: (i * nt + j, 0)))
    if rx_time_major:
        assert bb == 1
    return pl.pallas_call(
        _inproj_kernel,
        grid=(bn // bb, nt),
        in_specs=[pl.BlockSpec((bb, tb, d), lambda i, j: (i, j, 0)),
                  pl.BlockSpec((bb, 1, 6 * d), lambda i, j: (i, 0, 0)),
                  _const_spec((1, 1, d), lambda i, j: (layer, 0, 0)),
                  _const_spec((1, d, n_main), lambda i, j: (layer, 0, 0)),
                  _const_spec((1, d, SMALL_W), lambda i, j: (layer, 0, 0))],
        out_specs=out_specs,
        out_shape=out_shapes,
        compiler_params=_params(("arbitrary", "arbitrary")),
        name="in_proj",
    )(x, mod, g1, w_main, w_small)


def _gla_kernel(*refs, bb, tc, has_state):
    if has_state:
        gq, gk, gv, gg, sm, wf2, bf, gn, s0, o_ref, s_ref = refs
    else:
        gq, gk, gv, gg, sm, wf2, bf, gn, o_ref, s_ref = refs
        s0 = None
    c = CHUNK

    @pl.when(pl.program_id(1) == 0)
    def _():
        if has_state:
            s_ref[...] = s0[...]
        else:
            s_ref[...] = jnp.zeros_like(s_ref)

    ri = lax.broadcasted_iota(jnp.int32, (c, c), 0)
    ci = lax.broadcasted_iota(jnp.int32, (c, c), 1)
    causal = ri >= ci
    tril = causal.astype(F32)
    valid = lax.broadcasted_iota(jnp.int32, (c, DK), 0) < tc
    eye = (lax.broadcasted_iota(jnp.int32, (DK, DK), 0) == lax.broadcasted_iota(jnp.int32, (DK, DK), 1))
    nt_dims = (((1,), (1,)), ((), ()))
    tn_dims = (((0,), (0,)), ((), ()))
    for s in range(bb):
        r0 = s * tc
        smb = _pad_rows(sm[r0:r0 + tc, :], c).astype(BF16)
        for h in range(N_HEADS):
            kcols = slice(h * DK, (h + 1) * DK)
            vcols = slice(h * DV, (h + 1) * DV)
            q = _pad_rows(gq[r0:r0 + tc, kcols].astype(F32), c) * (DK ** -0.5)
            k = _pad_rows(gk[r0:r0 + tc, kcols].astype(F32), c)
            vb = _pad_rows(gv[r0:r0 + tc, vcols].astype(F32), c).astype(BF16)
            x = jnp.dot(smb, wf2[0, :, kcols], preferred_element_type=F32) + bf[0, :, kcols]
            logf = jnp.where(valid, _log_sigmoid(x) * (1.0 / GLA_GATE_TEMP), 0.0)
            b = jnp.dot(tril, logf, precision=HIGHEST, preferred_element_type=F32)
            b_last = b[c - 1:c, :]
            qd = (q * jnp.exp(b)).astype(BF16)
            kd = (k * jnp.exp(-b)).astype(BF16)
            att = lax.dot_general(qd, kd, nt_dims, preferred_element_type=F32)
            att = jnp.where(causal, att, 0.0).astype(BF16)
            st = s_ref[s, h]
            o = (jnp.dot(att, vb, preferred_element_type=F32)
                 + jnp.dot(qd, st.astype(BF16), preferred_element_type=F32))
            kend = (k * jnp.exp(b_last - b)).astype(BF16)
            dec = jnp.exp(_row_to_col(b_last, eye))
            s_ref[s, h] = dec * st + lax.dot_general(kend, vb, tn_dims, preferred_element_type=F32)
            o = o[:tc]
            g = gg[r0:r0 + tc, vcols].astype(F32)
            o_ref[r0:r0 + tc, vcols] = (_rms(o, gn[0]) * (g * _sigmoid(g))).astype(o_ref.dtype)


def _gla_mixer(p, small, wf2, bfv, gn, s0, layer, bn, t, bb, out_dtype):
    tc = min(CHUNK, t)
    nch = t // tc
    assert tc * nch == t and (bb == 1 or nch == 1)
    rows = bb * tc
    rspec = lambda w: pl.BlockSpec((rows, w), lambda i, j: (i * nch + j, 0))
    in_specs = [rspec(512), rspec(512), rspec(1024), rspec(1024), rspec(SMALL_W),
                _const_spec((1, SMALL_W, 512), lambda i, j: (layer, 0, 0)),
                _const_spec((1, 1, 512), lambda i, j: (layer, 0, 0)),
                _const_spec((1, 1, DV), lambda i, j: (layer, 0, 0))]
    args = [p["gq"], p["gk"], p["gv"], p["gg"], small, wf2, bfv, gn]
    sspec = pl.BlockSpec((bb, N_HEADS, DK, DV), lambda i, j: (i, 0, 0, 0))
    if s0 is not None:
        in_specs.append(pl.BlockSpec((None, bb, N_HEADS, DK, DV), lambda i, j: (layer, i, 0, 0, 0)))
        args.append(s0)
    return pl.pallas_call(
        functools.partial(_gla_kernel, bb=bb, tc=tc, has_state=s0 is not None),
        grid=(bn // bb, nch),
        in_specs=in_specs,
        out_specs=[rspec(1024), sspec],
        out_shape=[jax.ShapeDtypeStruct((bn * t, 1024), out_dtype),
                   jax.ShapeDtypeStruct((bn, N_HEADS, DK, DV), F32)],
        compiler_params=_params(("arbitrary", "arbitrary")),
        name="gla_mixer",
    )(*args)


def _mlstm_kernel(*refs, bb, tc, has_state):
    if has_state:
        mq, mk, mv, mo, sm, bias, gn, c0, n0, m0, o_ref, c_ref, n_ref, m_ref = refs
    else:
        mq, mk, mv, mo, sm, bias, gn, o_ref, c_ref, n_ref, m_ref = refs
    c = CHUNK

    @pl.when(pl.program_id(1) == 0)
    def _():
        if has_state:
            c_ref[...] = c0[...]
            n_ref[...] = n0[...]
            m_ref[...] = m0[...]
        else:
            c_ref[...] = jnp.zeros_like(c_ref)
            n_ref[...] = jnp.zeros_like(n_ref)
            m_ref[...] = jnp.zeros_like(m_ref)

    ri = lax.broadcasted_iota(jnp.int32, (c, c), 0)
    ci = lax.broadcasted_iota(jnp.int32, (c, c), 1)
    causal = ri >= ci
    eye = ri == ci
    tril = causal.astype(F32)
    valid = lax.broadcasted_iota(jnp.int32, (c, SMALL_W), 0) < tc
    lane = lax.broadcasted_iota(jnp.int32, (1, SMALL_W), 1)
    nt_dims = (((1,), (1,)), ((), ()))
    tn_dims = (((0,), (0,)), ((), ()))
    for s in range(bb):
        r0 = s * tc
        t1 = _pad_rows(sm[r0:r0 + tc, :], c) + bias[0]
        ig_t = jnp.where(valid, t1, NEG)
        lf_t = jnp.where(valid, _log_sigmoid(t1), 0.0)
        f_t = jnp.dot(tril, lf_t, precision=HIGHEST, preferred_element_type=F32)
        m_prev = m_ref[s]
        m_new_row = m_prev
        for h in range(N_HEADS):
            kcols = slice(h * DK, (h + 1) * DK)
            vcols = slice(h * DV, (h + 1) * DV)
            qb = _pad_rows(mq[r0:r0 + tc, kcols].astype(F32), c).astype(BF16)
            q = qb.astype(F32)
            k = _pad_rows(mk[r0:r0 + tc, kcols].astype(F32), c) * (DK ** -0.5)
            vb = _pad_rows(mv[r0:r0 + tc, vcols].astype(F32), c).astype(BF16)
            ic = ig_t[:, SM_I + h:SM_I + h + 1]
            fc = f_t[:, SM_F + h:SM_F + h + 1]
            ms = m_prev[:, h:h + 1]
            a = ic - fc
            a_row = _col_to_row(a, eye)
            cm = jnp.max(jnp.where(causal, jnp.broadcast_to(a_row, (c, c)), NEG), axis=1, keepdims=True)
            m = fc + jnp.maximum(ms, cm)
            w_inter = jnp.exp(fc + ms - m)
            dmat = jnp.exp(jnp.where(causal, (fc - m) + a_row, NEG))
            qk = lax.dot_general(qb, k.astype(BF16), nt_dims, preferred_element_type=F32) * dmat
            cs = c_ref[s, h]
            ns = n_ref[s, h:h + 1, :]
            num = (jnp.dot(qk.astype(BF16), vb, preferred_element_type=F32)
                   + w_inter * jnp.dot(qb, cs.astype(BF16), preferred_element_type=F32))
            den = jnp.sum(qk, axis=1, keepdims=True) + w_inter * jnp.sum(q * ns, axis=1, keepdims=True)
            hh = num / jnp.maximum(jnp.abs(den), jnp.exp(-m))
            m_last = m[c - 1:c, :]
            f_last = fc[c - 1:c, :]
            w_state = jnp.exp(ic + f_last - fc - m_last)
            carry = jnp.exp(f_last + ms - m_last)
            kw = k * w_state
            c_ref[s, h] = carry * cs + lax.dot_general(kw.astype(BF16), vb, tn_dims, preferred_element_type=F32)
            n_ref[s, h:h + 1, :] = carry * ns + jnp.sum(kw, axis=0, keepdims=True)
            m_new_row = jnp.where(lane == h, m_last, m_new_row)
            hh = hh[:tc]
            g = mo[r0:r0 + tc, vcols].astype(F32)
            o_ref[r0:r0 + tc, vcols] = (_rms(hh, gn[0]) * _sigmoid(g)).astype(o_ref.dtype)
        m_ref[s] = m_new_row


def _mlstm_mixer(p, small, bias, gn, state, layer, bn, t, bb, out_dtype):
    tc = min(CHUNK, t)
    nch = t // tc
    assert tc * nch == t and (bb == 1 or nch == 1)
    rows = bb * tc
    rspec = lambda w: pl.BlockSpec((rows, w), lambda i, j: (i * nch + j, 0))
    in_specs = [rspec(512), rspec(512), rspec(1024), rspec(1024), rspec(SMALL_W),
                _const_spec((1, 1, SMALL_W), lambda i, j: (layer, 0, 0)),
                _const_spec((1, 1, DV), lambda i, j: (layer, 0, 0))]
    args = [p["mq"], p["mk"], p["mv"], p["mo"], small, bias, gn]
    cspec = pl.BlockSpec((bb, N_HEADS, DK, DV), lambda i, j: (i, 0, 0, 0))
    nspec = pl.BlockSpec((bb, N_HEADS, DK), lambda i, j: (i, 0, 0))
    mspec = pl.BlockSpec((bb, 1, SMALL_W), lambda i, j: (i, 0, 0))
    if state is not None:
        c0, n0, m0 = state
        in_specs += [pl.BlockSpec((None, bb, N_HEADS, DK, DV), lambda i, j: (layer, i, 0, 0, 0)),
                     pl.BlockSpec((None, bb, N_HEADS, DK), lambda i, j: (layer, i, 0, 0)),
                     pl.BlockSpec((None, bb, 1, SMALL_W), lambda i, j: (layer, i, 0, 0))]
        args += [c0, n0, m0]
    return pl.pallas_call(
        functools.partial(_mlstm_kernel, bb=bb, tc=tc, has_state=state is not None),
        grid=(bn // bb, nch),
        in_specs=in_specs,
        out_specs=[rspec(1024), cspec, nspec, mspec],
        out_shape=[jax.ShapeDtypeStruct((bn * t, 1024), out_dtype),
                   jax.ShapeDtypeStruct((bn, N_HEADS, DK, DV), F32),
                   jax.ShapeDtypeStruct((bn, N_HEADS, DK), F32),
                   jax.ShapeDtypeStruct((bn, 1, SMALL_W), F32)],
        compiler_params=_params(("arbitrary", "arbitrary")),
        name="mlstm_mixer",
    )(*args)


def _rglru_kernel(*refs, nb, tc, has_state):
    if has_state:
        rx, cb, h0, cw, cbias, wa, ba, wx, bx, lam, hs_ref, hf_ref, nb_ref, xpad, a_s, u_s = refs
    else:
        rx, cw, cbias, wa, ba, wx, bx, lam, hs_ref, hf_ref, nb_ref, xpad, a_s, u_s = refs
    hist = (CONV_W - 1) * nb
    rows = tc * nb
    w = rx.shape[1]
    bw = w // RG_BLOCKS

    @pl.when(pl.program_id(0) == 0)
    def _():
        if has_state:
            xpad[0:hist, :] = cb[0]
            hf_ref[...] = h0[0]
        else:
            xpad[0:hist, :] = jnp.zeros((hist, w), F32)
            hf_ref[...] = jnp.zeros_like(hf_ref)

    xpad[hist:hist + rows, :] = rx[...].astype(F32)
    xc = cbias[0]
    for j in range(CONV_W):
        xc = xc + xpad[j * nb:j * nb + rows, :] * cw[0, j:j + 1, :]
    sp = _softplus(-lam[0])
    for n in range(RG_BLOCKS):
        cols = slice(n * bw, (n + 1) * bw)
        xn = xc[:, cols]
        xg = xn.astype(BF16)
        r = _sigmoid(jnp.dot(xg, wa[0, n], preferred_element_type=F32) + ba[0, :, cols])
        i = _sigmoid(jnp.dot(xg, wx[0, n], preferred_element_type=F32) + bx[0, :, cols])
        log_a = -RG_C * r * sp[:, cols]
        a_s[:, cols] = jnp.exp(log_a)
        u_s[:, cols] = jnp.sqrt(1.0 - jnp.exp(2.0 * log_a)) * (i * xn)

    def body(t, h):
        o = pl.multiple_of(t * nb, nb)
        h = a_s[pl.ds(o, nb), :] * h + u_s[pl.ds(o, nb), :]
        hs_ref[pl.ds(o, nb), :] = h.astype(hs_ref.dtype)
        return h

    hf_ref[...] = lax.fori_loop(0, tc, body, hf_ref[...])
    tail = xpad[rows:rows + hist, :]
    xpad[0:hist, :] = tail
    nb_ref[...] = tail


def _rglru(rx, conv_buf, h0, wts, layer, nb, t, tc, out_dtype):
    w = rx.shape[1]
    rows = tc * nb
    hist = (CONV_W - 1) * nb
    bw = w // RG_BLOCKS
    lsp = lambda shape: _const_spec((1,) + shape, lambda j: (layer,) + (0,) * len(shape))
    in_specs = [pl.BlockSpec((rows, w), lambda j: (j, 0))]
    args = [rx]
    if conv_buf is not None:
        in_specs += [lsp((hist, w)), lsp((nb, w))]
        args += [conv_buf, h0]
    in_specs += [lsp((CONV_W, w)), lsp((1, w)), lsp((RG_BLOCKS, bw, bw)), lsp((1, w)),
                 lsp((RG_BLOCKS, bw, bw)), lsp((1, w)), lsp((1, w))]
    args += [wts["conv_w"], wts["conv_b"], wts["w_rg_a"], wts["b_rg_a"], wts["w_rg_x"], wts["b_rg_x"],
             wts["rg_lambda"]]
    return pl.pallas_call(
        functools.partial(_rglru_kernel, nb=nb, tc=tc, has_state=conv_buf is not None),
        grid=(t // tc,),
        in_specs=in_specs,
        out_specs=[pl.BlockSpec((rows, w), lambda j: (j, 0)),
                   pl.BlockSpec((nb, w), lambda j: (0, 0)),
                   pl.BlockSpec((hist, w), lambda j: (0, 0))],
        out_shape=[jax.ShapeDtypeStruct((t * nb, w), out_dtype),
                   jax.ShapeDtypeStruct((nb, w), F32),
                   jax.ShapeDtypeStruct((hist, w), F32)],
        scratch_shapes=[pltpu.VMEM((rows + hist, w), F32), pltpu.VMEM((rows, w), F32), pltpu.VMEM((rows, w), F32)],
        compiler_params=_params(("arbitrary",)),
        name="rglru",
    )(*args)


def _merge_kernel(og, om, orr, mg, x_ref, mod_ref, g2, wbg, wbm, wbr, wo, y_ref, h2_ref):
    bb, tb, d = x_ref.shape
    gates = mg[...].astype(F32)
    merged = (_sigmoid(gates[:, 0:d]) * jnp.dot(og[...].astype(BF16), wbg[0], preferred_element_type=F32)
              + _sigmoid(gates[:, d:2 * d]) * jnp.dot(om[...].astype(BF16), wbm[0], preferred_element_type=F32)
              + _sigmoid(gates[:, 2 * d:3 * d]) * jnp.dot(orr[...].astype(BF16), wbr[0], preferred_element_type=F32))
    z = jnp.dot(merged.astype(BF16), wo[0], preferred_element_type=F32)
    mod = mod_ref[...]
    y = x_ref[...] + mod[:, :, 2 * d:3 * d] * z.reshape(bb, tb, d)
    y_ref[...] = y
    h2 = _rms(y, g2[...]) * (1.0 + mod[:, :, 4 * d:5 * d]) + mod[:, :, 3 * d:4 * d]
    h2_ref[...] = h2.reshape(bb * tb, d).astype(BF16)


def _merge(og, om, orr, mg, x, mod, g2, wts, layer, bb, tb, rg_time_major):
    bn, t, d = x.shape
    nt = t // tb
    rows = bb * tb
    rspec = lambda w: pl.BlockSpec((rows, w), lambda i, j: (i * nt + j, 0))
    if rg_time_major:
        assert bb == 1
        rg_spec = pl.BlockSpec((tb, d), lambda i, j: (j, i))
    else:
        rg_spec = rspec(d)
    wspec = _const_spec((1, d, d), lambda i, j: (layer, 0, 0))
    return pl.pallas_call(
        _merge_kernel,
        grid=(bn // bb, nt),
        in_specs=[rspec(d), rspec(d), rg_spec, rspec(3 * d),
                  pl.BlockSpec((bb, tb, d), lambda i, j: (i, j, 0)),
                  pl.BlockSpec((bb, 1, 6 * d), lambda i, j: (i, 0, 0)),
                  _const_spec((1, 1, d), lambda i, j: (layer, 0, 0)),
                  wspec, wspec, wspec, wspec],
        out_specs=[pl.BlockSpec((bb, tb, d), lambda i, j: (i, j, 0)), rspec(d)],
        out_shape=[jax.ShapeDtypeStruct((bn, t, d), F32), jax.ShapeDtypeStruct((bn * t, d), BF16)],
        compiler_params=_params(("arbitrary", "arbitrary")),
        name="merge",
    )(og, om, orr, mg, x, mod, g2, wts["w_br_gla"], wts["w_br_ml"], wts["w_br_rg"], wts["w_out"])


def _sort16_pairs():
    n, pairs, p = 16, [], 1
    while p < n:
        k = p
        while k >= 1:
            for j in range(k % p, n - k, 2 * k):
                for i in range(min(k, n - j - k)):
                    if (i + j) // (2 * p) == (i + j + k) // (2 * p):
                        pairs.append((i + j, i + j + k))
            k //= 2
        p *= 2
    return pairs


_SORT16 = _sort16_pairs()
_BITONIC16 = [(i, i + dd) for dd in (8, 4, 2, 1) for i in range(16) if (i & dd) == 0]
_CANDS = [(a, b) for a in range(PEER_TOPK) for b in range(PEER_TOPK) if (a + 1) * (b + 1) <= PEER_TOPK]


def _top16_sorted(st):
    xs = [st[g * 8:(g + 1) * 8, :] for g in range(16)]
    for i, j in _SORT16:
        hi, lo = jnp.maximum(xs[i], xs[j]), jnp.minimum(xs[i], xs[j])
        xs[i], xs[j] = hi, lo
    for shift in (4, 2, 1):
        ys = [jnp.maximum(xs[r], pltpu.roll(xs[15 - r], shift, 0)) for r in range(16)]
        for i, j in _BITONIC16:
            hi, lo = jnp.maximum(ys[i], ys[j]), jnp.minimum(ys[i], ys[j])
            ys[i], ys[j] = hi, lo
        xs = ys
    return xs


def _peer_kernel(h2_ref, y_ref, mod_ref, wq_ref, keys_ref, u_ref, vt_ref, fg_ref, o_ref,
                 s0_ref, s1_ref, cc_ref, e1_ref, th_ref, pt_ref, gt_ref, acc_ref, *, final, sub):
    bb, tb, d = y_ref.shape
    tm = bb * tb
    ec = u_ref.shape[1]
    cidx = pl.program_id(1)
    nt_dims = (((1,), (1,)), ((), ()))

    @pl.when(cidx == 0)
    def _():
        q = jnp.dot(h2_ref[...], wq_ref[0], preferred_element_type=F32)
        sub8 = lax.broadcasted_iota(jnp.int32, (8, tm), 0)
        tops = []
        for p in range(2):
            stack = [jnp.zeros((8, tm), F32) for _ in range(PEER_TOPK)]
            for h in range(PEER_HEADS):
                col = (h * 2 + p) * PEER_NKEYS
                qs = q[:, col:col + PEER_NKEYS].astype(BF16)
                st = lax.dot_general(keys_ref[0, h, p], qs, nt_dims, preferred_element_type=F32)
                (s0_ref if p == 0 else s1_ref)[h] = st
                srt = _top16_sorted(st)
                stack = [jnp.where(sub8 == h, srt[r], stack[r]) for r in range(PEER_TOPK)]
            tops.append(stack)
        cands = [tops[0][a] + tops[1][b] for a, b in _CANDS]
        work = list(cands)
        theta = jnp.full((8, tm), NEG, F32)
        cnt = jnp.zeros((8, tm), F32)
        for _ in range(PEER_TOPK):
            mx = functools.reduce(jnp.maximum, work)
            eq = [w == mx for w in work]
            theta = jnp.where(cnt < PEER_TOPK, mx, theta)
            cnt = cnt + functools.reduce(jnp.add, [e.astype(F32) for e in eq])
            work = [jnp.where(e, NEG, w) for e, w in zip(eq, work)]
        top = tops[0][0] + tops[1][0]
        z = functools.reduce(jnp.add, [jnp.where(cd >= theta, jnp.exp(cd - top), 0.0) for cd in cands])
        th_ref[...] = theta
        inv_z = 1.0 / z
        for h in range(PEER_HEADS):
            cc_ref[h] = jnp.exp(s0_ref[h] - tops[0][0][h:h + 1, :]) * inv_z[h:h + 1, :]
            e1_ref[h] = jnp.exp(s1_ref[h] - tops[1][0][h:h + 1, :])
        acc_ref[...] = jnp.zeros_like(acc_ref)

    pt_ref[...] = lax.dot_general(u_ref[0], h2_ref[...], nt_dims, preferred_element_type=F32)
    groups = ec // PEER_NKEYS
    nsub = PEER_NKEYS // sub

    def body(n, carry):
        g = n // nsub
        jb = n % nsub
        i = cidx * groups + g
        r0 = pl.multiple_of(n * sub, sub)
        j0 = pl.multiple_of(jb * sub, sub)
        pre = pt_ref[pl.ds(r0, sub), :]
        wgt = jnp.zeros((sub, tm), F32)
        for h in range(PEER_HEADS):
            zz = s1_ref[h, pl.ds(j0, sub), :] + s0_ref[h, pl.ds(i, 1), :]
            wgt = wgt + jnp.where(zz >= th_ref[h:h + 1, :],
                                  e1_ref[h, pl.ds(j0, sub), :] * cc_ref[h, pl.ds(i, 1), :], 0.0)
        act = 0.5 * pre * (1.0 + lax.erf(pre * 0.7071067811865476))
        gt_ref[pl.ds(r0, sub), :] = (act * wgt).astype(BF16)
        return carry

    lax.fori_loop(0, groups * nsub, body, 0)
    acc_ref[...] += jnp.dot(vt_ref[0], gt_ref[...], preferred_element_type=F32)

    @pl.when(cidx == pl.num_programs(1) - 1)
    def _():
        peer = acc_ref[...].T.reshape(bb, tb, d)
        y = y_ref[...] + mod_ref[...][:, :, 5 * d:6 * d] * peer
        if final:
            y = _rms(y, fg_ref[...])
        o_ref[...] = y


def _peer(h2, y, mod, wts, final_g, layer, bb, tb, ec, final):
    bn, t, d = y.shape
    nt = t // tb
    tm = bb * tb
    n_exp = wts["peer_u"].shape[1]
    nq = wts["w_peer_q"].shape[-1]
    sub = 16
    tile = lambda: pltpu.VMEM((PEER_HEADS, PEER_NKEYS, tm), F32)
    return pl.pallas_call(
        functools.partial(_peer_kernel, final=final, sub=sub),
        grid=(bn * t // tm, n_exp // ec),
        in_specs=[pl.BlockSpec((tm, d), lambda i, c: (i, 0)),
                  pl.BlockSpec((bb, tb, d), lambda i, c: (i // nt, i % nt, 0)),
                  pl.BlockSpec((bb, 1, 6 * d), lambda i, c: (i // nt, 0, 0)),
                  _const_spec((1, d, nq), lambda i, c: (layer, 0, 0)),
                  _const_spec((1, PEER_HEADS, 2, PEER_NKEYS, PEER_NKEYS), lambda i, c: (layer, 0, 0, 0, 0)),
                  pl.BlockSpec((1, ec, d), lambda i, c: (layer, c, 0)),
                  pl.BlockSpec((1, d, ec), lambda i, c: (layer, 0, c)),
                  _const_spec((1, 1, d), lambda i, c: (0, 0, 0))],
        out_specs=pl.BlockSpec((bb, tb, d), lambda i, c: (i // nt, i % nt, 0)),
        out_shape=jax.ShapeDtypeStruct((bn, t, d), F32),
        scratch_shapes=[tile(), tile(), tile(), tile(),
                        pltpu.VMEM((PEER_HEADS, tm), F32),
                        pltpu.VMEM((ec, tm), F32),
                        pltpu.VMEM((ec, tm), BF16),
                        pltpu.VMEM((d, tm), F32)],
        compiler_params=_params(("arbitrary", "arbitrary")),
        name="peer",
    )(h2, y, mod, wts["w_peer_q"], wts["peer_keys"], wts["peer_u"], wts["peer_vt"], final_g)


def _prepare_weights(w_in, w_gla_f2, b_gla_f, gla_norm_g, b_ml_i, b_ml_f, ml_norm_g, conv_w, conv_b, w_rg_a, b_rg_a,
                     w_rg_x, b_rg_x, rg_lambda, w_br_gla, w_br_ml, w_br_rg, w_out, w_peer_q, peer_keys, peer_u, peer_v,
                     norm1_g, norm2_g):
    n_layers, d, _ = w_in.shape
    offs, o = {}, 0
    for name, w in (("gq", 512), ("gk", 512), ("gv", 1024), ("gg", 1024), ("gf", GLA_RANK),
                    ("mq", 512), ("mk", 512), ("mv", 1024), ("mo", 1024), ("mi", N_HEADS), ("mf", N_HEADS),
                    ("rx", 1024), ("mg", 3072)):
        offs[name] = (o, o + w)
        o += w
    assert o == w_in.shape[-1]
    cut = lambda name: w_in[:, :, offs[name][0]:offs[name][1]]
    w_main = jnp.concatenate([cut(n) for n, _ in IN_PIECES], axis=-1).astype(BF16)
    n_small = GLA_RANK + 2 * N_HEADS
    w_small = jnp.concatenate([cut("gf"), cut("mi"), cut("mf"),
                               jnp.zeros((n_layers, d, SMALL_W - n_small), F32)], axis=-1).astype(BF16)
    wf2 = jnp.concatenate([w_gla_f2, jnp.zeros((n_layers, SMALL_W - GLA_RANK, w_gla_f2.shape[-1]), F32)],
                          axis=1).astype(BF16)
    bias_small = jnp.concatenate([jnp.zeros((n_layers, GLA_RANK), F32), b_ml_i, b_ml_f,
                                  jnp.zeros((n_layers, SMALL_W - n_small), F32)], axis=-1)
    row = lambda a: a.reshape(n_layers, 1, a.shape[-1])
    return dict(
        w_main=w_main, w_small=w_small, wf2=wf2, b_gla_f=row(b_gla_f), gla_norm_g=row(gla_norm_g),
        bias_small=row(bias_small), ml_norm_g=row(ml_norm_g),
        conv_w=conv_w, conv_b=row(conv_b), w_rg_a=w_rg_a.astype(BF16), b_rg_a=row(b_rg_a),
        w_rg_x=w_rg_x.astype(BF16), b_rg_x=row(b_rg_x), rg_lambda=row(rg_lambda),
        w_br_gla=w_br_gla.astype(BF16), w_br_ml=w_br_ml.astype(BF16), w_br_rg=w_br_rg.astype(BF16),
        w_out=w_out.astype(BF16), w_peer_q=w_peer_q.astype(BF16), peer_keys=peer_keys.astype(BF16),
        peer_u=peer_u.astype(BF16), peer_vt=jnp.swapaxes(peer_v, 1, 2).astype(BF16),
        norm1_g=row(norm1_g), norm2_g=row(norm2_g))


def _layer(x, mod, state, wts, final_g, layer, final, cfg):
    bn, t, d = x.shape
    fresh = state is None
    act_dtype = BF16 if fresh else F32
    outs = _in_projection(x, mod, wts["norm1_g"], wts["w_main"], wts["w_small"], layer,
                          cfg["bb"], cfg["tb"], act_dtype, rx_time_major=fresh)
    p = {name: o for (name, _), o in zip(IN_PIECES, outs[:-1])}
    small = outs[-1]
    if fresh:
        s_gla = ml_state = None
        rx = p["rx"].reshape(t * bn, d)
        conv_buf = h0 = None
    else:
        s_gla, c0, n0, m0, h0, conv_buf = state
        ml_state = (c0, n0, m0)
        rx = jnp.swapaxes(p["rx"].reshape(bn, t, d), 0, 1).reshape(t * bn, d)
    o_gla, s_gla_new = _gla_mixer(p, small, wts["wf2"], wts["b_gla_f"], wts["gla_norm_g"], s_gla, layer,
                                  bn, t, cfg["mix_bb"], act_dtype)
    o_ml, c_new, n_new, m_new = _mlstm_mixer(p, small, wts["bias_small"], wts["ml_norm_g"], ml_state, layer,
                                             bn, t, cfg["mix_bb"], act_dtype)
    hs, h_fin, buf_new = _rglru(rx, conv_buf, h0, wts, layer, bn, t, cfg["rg_tc"], act_dtype)
    if fresh:
        o_rg = hs.reshape(t, bn * d)
    else:
        o_rg = jnp.swapaxes(hs.reshape(t, bn, d), 0, 1).reshape(bn * t, d)
    y, h2 = _merge(o_gla, o_ml, o_rg, p["mg"], x, mod, wts["norm2_g"], wts, layer, cfg["bb"], cfg["tb"],
                   rg_time_major=fresh)
    out = _peer(h2, y, mod, wts, final_g, layer, cfg["peer_bb"], cfg["peer_tb"], cfg["peer_ec"], final)
    new_state = (s_gla_new, c_new, n_new, m_new[:, 0, :N_HEADS], h_fin,
                 jnp.swapaxes(buf_new.reshape(CONV_W - 1, bn, d), 0, 1))
    return out, new_state


def kernel(x_prompt, x_sample, state_gla, state_mlstm_c, state_mlstm_n, state_mlstm_m, state_rglru_h, state_rglru_conv, c_prompt, c_sample, w_ada, b_ada, norm1_g, norm2_g, w_in, w_gla_f2, b_gla_f, gla_norm_g, b_ml_i, b_ml_f, ml_norm_g, conv_w, conv_b, w_rg_a, b_rg_a, w_rg_x, b_rg_x, rg_lambda, w_br_gla, w_br_ml, w_br_rg, w_out, w_peer_q, peer_keys, peer_u, peer_v, final_norm_g):
    n_layers = w_in.shape[0]
    bp, tp, d = x_prompt.shape
    bs, ts, _ = x_sample.shape
    wts = _prepare_weights(w_in, w_gla_f2, b_gla_f, gla_norm_g, b_ml_i, b_ml_f, ml_norm_g, conv_w, conv_b, w_rg_a,
                           b_rg_a, w_rg_x, b_rg_x, rg_lambda, w_br_gla, w_br_ml, w_br_rg, w_out, w_peer_q,
                           peer_keys, peer_u, peer_v, norm1_g, norm2_g)
    final_g = final_norm_g.reshape(1, 1, d)
    mod_all = _modulation(jnp.concatenate([c_prompt, c_sample], axis=0), w_ada, b_ada)
    m0_pad = jnp.pad(state_mlstm_m, ((0, 0), (0, 0), (0, SMALL_W - N_HEADS))).reshape(n_layers, bs, 1, SMALL_W)
    conv_tm = jnp.swapaxes(state_rglru_conv, 1, 2).reshape(n_layers, (CONV_W - 1) * bs, d)
    sample_state = (state_gla, state_mlstm_c, state_mlstm_n, m0_pad, state_rglru_h, conv_tm)

    cfg_p = dict(bb=1, tb=min(256, tp), mix_bb=1, rg_tc=min(128, tp), peer_bb=1, peer_tb=min(512, tp), peer_ec=1024)
    sb = min(32, bs)
    cfg_s = dict(bb=min(16, bs), tb=ts, mix_bb=min(4, bs), rg_tc=ts, peer_bb=min(512 // ts, bs), peer_tb=ts,
                 peer_ec=1024)
    yp, ys = x_prompt, x_sample
    new_p, new_s = [], []
    for l in range(n_layers):
        final = l == n_layers - 1
        mod_p = mod_all[l, :bp].reshape(bp, 1, 6 * d)
        mod_s = mod_all[l, bp:].reshape(bs, 1, 6 * d)
        yp, st_p = _layer(yp, mod_p, None, wts, final_g, l, final, cfg_p)
        ys, st_s = _layer(ys, mod_s, sample_state, wts, final_g, l, final, cfg_s)
        new_p.append(st_p)
        new_s.append(st_s)
    stacked_p = [jnp.stack(z) for z in zip(*new_p)]
    stacked_s = [jnp.stack(z) for z in zip(*new_s)]
    return (yp, ys, *stacked_p, *stacked_s)
```

```python
import functools

import jax
import jax.numpy as jnp
from jax import lax
from jax.experimental import pallas as pl
from jax.experimental.pallas import tpu as pltpu

F32 = jnp.float32
BF16 = jnp.bfloat16
HIGHEST = lax.Precision.HIGHEST

EPS = 1e-6
N_HEADS = 4
DK = 128
DV = 256
GLA_RANK = 16
GLA_GATE_TEMP = 16.0
CHUNK = 64
RG_BLOCKS = 8
RG_C = 8.0
CONV_W = 4
PEER_HEADS = 8
PEER_NKEYS = 128
PEER_TOPK = 16
NEG = -1e30
LANES = 128

VMEM_LIMIT_BYTES = 56 * 1024 * 1024

IN_PIECES = (("gq", 512), ("gk", 512), ("gv", 1024), ("gg", 1024),
             ("mq", 512), ("mk", 512), ("mv", 1024), ("mo", 1024),
             ("rx", 1024), ("mg", 3072))
SMALL_W = 128
SM_I = GLA_RANK
SM_F = GLA_RANK + N_HEADS


def _params(sem):
    return pltpu.CompilerParams(dimension_semantics=sem, vmem_limit_bytes=VMEM_LIMIT_BYTES)


def _const_spec(shape, index_map):
    return pl.BlockSpec(shape, index_map, pipeline_mode=pl.Buffered(1))


def _log_sigmoid(x):
    return jnp.minimum(x, 0.0) - jnp.log1p(jnp.exp(-jnp.abs(x)))


def _softplus(x):
    return jnp.maximum(x, 0.0) + jnp.log1p(jnp.exp(-jnp.abs(x)))


def _sigmoid(x):
    return 1.0 / (1.0 + jnp.exp(-x))


def _rms(x, g):
    return x * lax.rsqrt(jnp.mean(x * x, axis=-1, keepdims=True) + EPS) * g


def _pad_rows(a, rows, value=0.0):
    if a.shape[0] == rows:
        return a
    return jnp.concatenate([a, jnp.full((rows - a.shape[0], a.shape[1]), value, a.dtype)], axis=0)


def _row_to_col(row, eye):
    n = row.shape[1]
    return jnp.sum(jnp.where(eye, jnp.broadcast_to(row, (n, n)), 0.0), axis=1, keepdims=True)


def _col_to_row(col, eye):
    n = col.shape[0]
    return jnp.sum(jnp.where(eye, jnp.broadcast_to(col, (n, n)), 0.0), axis=0, keepdims=True)


def _mod_kernel(c_ref, w_ref, b_ref, o_ref):
    c = c_ref[...]
    s = (c * _sigmoid(c)).astype(BF16)
    o_ref[0] = jnp.dot(s, w_ref[0].astype(BF16), preferred_element_type=F32) + b_ref[0]


def _modulation(c_all, w_ada, b_ada):
    n_layers, d, n6 = w_ada.shape
    rows = c_all.shape[0]
    tn = 1536
    return pl.pallas_call(
        _mod_kernel,
        grid=(n_layers, n6 // tn),
        in_specs=[pl.BlockSpec((rows, d), lambda l, n: (0, 0)),
                  pl.BlockSpec((1, d, tn), lambda l, n: (l, 0, n)),
                  pl.BlockSpec((1, 1, tn), lambda l, n: (l, 0, n))],
        out_specs=pl.BlockSpec((1, rows, tn), lambda l, n: (l, 0, n)),
        out_shape=jax.ShapeDtypeStruct((n_layers, rows, n6), F32),
        compiler_params=_params(("arbitrary", "arbitrary")),
        name="adaln_mod",
    )(c_all, w_ada, b_ada.reshape(n_layers, 1, n6))


def _inproj_kernel(x_ref, mod_ref, g_ref, wm_ref, ws_ref, *out_refs):
    bb, tb, d = x_ref.shape
    mod = mod_ref[...]
    h = _rms(x_ref[...], g_ref[...]) * (1.0 + mod[:, :, d:2 * d]) + mod[:, :, 0:d]
    hb = h.reshape(bb * tb, d).astype(BF16)
    off = 0
    for r, (_, w) in zip(out_refs[:-1], IN_PIECES):
        r[...] = jnp.dot(hb, wm_ref[0, :, off:off + w], preferred_element_type=F32).astype(r.dtype)
        off += w
    out_refs[-1][...] = jnp.dot(hb, ws_ref[0], preferred_element_type=F32)


def _in_projection(x, mod, g1, w_main, w_small, layer, bb, tb, out_dtype, rx_time_major):
    bn, t, d = x.shape
    nt = t // tb
    rows = bb * tb
    n_main = w_main.shape[-1]
    out_shapes, out_specs = [], []
    for name, w in IN_PIECES:
        if name == "rx" and rx_time_major:
            out_shapes.append(jax.ShapeDtypeStruct((t, bn * w), out_dtype))
            out_specs.append(pl.BlockSpec((tb, w), lambda i, j: (j, i)))
        else:
            out_shapes.append(jax.ShapeDtypeStruct((bn * t, w), out_dtype))
            out_specs.append(pl.BlockSpec((rows, w), lambda i, j: (i * nt + j, 0)))
    out_shapes.append(jax.ShapeDtypeStruct((bn * t, SMALL_W), F32))
    out_specs.append(pl.BlockSpec((rows, SMALL_W), lambda i, j: (i * nt + j, 0)))
    if rx_time_major:
        assert bb == 1
    return pl.pallas_call(
        _inproj_kernel,
        grid=(bn // bb, nt),
        in_specs=[pl.BlockSpec((bb, tb, d), lambda i, j: (i, j, 0)),
                  pl.BlockSpec((bb, 1, 6 * d), lambda i, j: (i, 0, 0)),
                  _const_spec((1, 1, d), lambda i, j: (layer, 0, 0)),
                  _const_spec((1, d, n_main), lambda i, j: (layer, 0, 0)),
                  _const_spec((1, d, SMALL_W), lambda i, j: (layer, 0, 0))],
        out_specs=out_specs,
        out_shape=out_shapes,
        compiler_params=_params(("arbitrary", "arbitrary")),
        name="in_proj",
    )(x, mod, g1, w_main, w_small)


def _gla_kernel(*refs, bb, tc, has_state):
    if has_state:
        gq, gk, gv, gg, sm, wf2, bf, gn, s0, o_ref, s_ref = refs
    else:
        gq, gk, gv, gg, sm, wf2, bf, gn, o_ref, s_ref = refs
        s0 = None
    c = CHUNK

    @pl.when(pl.program_id(1) == 0)
    def _():
        if has_state:
            s_ref[...] = s0[...]
        else:
            s_ref[...] = jnp.zeros_like(s_ref)

    ri = lax.broadcasted_iota(jnp.int32, (c, c), 0)
    ci = lax.broadcasted_iota(jnp.int32, (c, c), 1)
    causal = ri >= ci
    tril = causal.astype(F32)
    hk = N_HEADS * DK
    valid = lax.broadcasted_iota(jnp.int32, (c, hk), 0) < tc
    eye = (lax.broadcasted_iota(jnp.int32, (DK, DK), 0) == lax.broadcasted_iota(jnp.int32, (DK, DK), 1))
    nt_dims = (((1,), (1,)), ((), ()))
    tn_dims = (((0,), (0,)), ((), ()))
    for s in range(bb):
        smb = _pad_rows(sm[s], c).astype(BF16)
        x = jnp.dot(smb, wf2[0], preferred_element_type=F32) + bf[0]
        logf = jnp.where(valid, _log_sigmoid(x) * (1.0 / GLA_GATE_TEMP), 0.0)
        b = jnp.dot(tril, logf, precision=HIGHEST, preferred_element_type=F32)
        b_last = b[c - 1:c, :]
        q = _pad_rows(gq[s].astype(F32), c) * (DK ** -0.5)
        k = _pad_rows(gk[s].astype(F32), c)
        qd = (q * jnp.exp(b)).astype(BF16)
        kd = (k * jnp.exp(-b)).astype(BF16)
        kend = (k * jnp.exp(b_last - b)).astype(BF16)
        vb = _pad_rows(gv[s].astype(F32), c).astype(BF16)
        g = gg[s].astype(F32)
        g = g * _sigmoid(g)
        for h in range(N_HEADS):
            kcols = slice(h * DK, (h + 1) * DK)
            vcols = slice(h * DV, (h + 1) * DV)
            att = lax.dot_general(qd[:, kcols], kd[:, kcols], nt_dims, preferred_element_type=F32)
            att = jnp.where(causal, att, 0.0).astype(BF16)
            st = s_ref[s, h]
            o = (jnp.dot(att, vb[:, vcols], preferred_element_type=F32)
                 + jnp.dot(qd[:, kcols], st.astype(BF16), preferred_element_type=F32))
            dec = jnp.exp(_row_to_col(b_last[:, kcols], eye))
            s_ref[s, h] = dec * st + lax.dot_general(kend[:, kcols], vb[:, vcols], tn_dims,
                                                     preferred_element_type=F32)
            o_ref[s, :, vcols] = (_rms(o[:tc], gn[0]) * g[:, vcols]).astype(o_ref.dtype)


def _gla_mixer(p, small, wf2, bfv, gn, s0, layer, bn, t, bb, out_dtype):
    tc = min(CHUNK, t)
    nch = t // tc
    assert tc * nch == t
    rspec = lambda w: pl.BlockSpec((bb, tc, w), lambda i, j: (i, j, 0))
    r3 = lambda a: a.reshape(bn, t, a.shape[-1])
    in_specs = [rspec(512), rspec(512), rspec(1024), rspec(1024), rspec(SMALL_W),
                _const_spec((1, SMALL_W, 512), lambda i, j: (layer, 0, 0)),
                _const_spec((1, 1, 512), lambda i, j: (layer, 0, 0)),
                _const_spec((1, 1, DV), lambda i, j: (layer, 0, 0))]
    args = [r3(p["gq"]), r3(p["gk"]), r3(p["gv"]), r3(p["gg"]), r3(small), wf2, bfv, gn]
    sspec = pl.BlockSpec((bb, N_HEADS, DK, DV), lambda i, j: (i, 0, 0, 0))
    if s0 is not None:
        in_specs.append(pl.BlockSpec((None, bb, N_HEADS, DK, DV), lambda i, j: (layer, i, 0, 0, 0)))
        args.append(s0)
    o, s_new = pl.pallas_call(
        functools.partial(_gla_kernel, bb=bb, tc=tc, has_state=s0 is not None),
        grid=(bn // bb, nch),
        in_specs=in_specs,
        out_specs=[rspec(1024), sspec],
        out_shape=[jax.ShapeDtypeStruct((bn, t, 1024), out_dtype),
                   jax.ShapeDtypeStruct((bn, N_HEADS, DK, DV), F32)],
        compiler_params=_params(("arbitrary", "arbitrary")),
        name="gla_mixer",
    )(*args)
    return o.reshape(bn * t, 1024), s_new


def _mlstm_kernel(*refs, bb, tc, has_state):
    if has_state:
        mq, mk, mv, mo, sm, bias, gn, c0, n0, m0, o_ref, c_ref, n_ref, m_ref = refs
    else:
        mq, mk, mv, mo, sm, bias, gn, o_ref, c_ref, n_ref, m_ref = refs
    c = CHUNK

    @pl.when(pl.program_id(1) == 0)
    def _():
        if has_state:
            c_ref[...] = c0[...]
            n_ref[...] = n0[...]
            m_ref[...] = m0[...]
        else:
            c_ref[...] = jnp.zeros_like(c_ref)
            n_ref[...] = jnp.zeros_like(n_ref)
            m_ref[...] = jnp.zeros_like(m_ref)

    ri = lax.broadcasted_iota(jnp.int32, (c, c), 0)
    ci = lax.broadcasted_iota(jnp.int32, (c, c), 1)
    causal = ri >= ci
    eye = ri == ci
    tril = causal.astype(F32)
    valid = lax.broadcasted_iota(jnp.int32, (c, SMALL_W), 0) < tc
    lane = lax.broadcasted_iota(jnp.int32, (1, SMALL_W), 1)
    nt_dims = (((1,), (1,)), ((), ()))
    tn_dims = (((0,), (0,)), ((), ()))
    for s in range(bb):
        t1 = _pad_rows(sm[s], c) + bias[0]
        ig_t = jnp.where(valid, t1, NEG)
        lf_t = jnp.where(valid, _log_sigmoid(t1), 0.0)
        f_t = jnp.dot(tril, lf_t, precision=HIGHEST, preferred_element_type=F32)
        m_prev = m_ref[s]
        m_new_row = m_prev
        qb_all = _pad_rows(mq[s].astype(F32), c).astype(BF16)
        k_all = _pad_rows(mk[s].astype(F32), c) * (DK ** -0.5)
        vb_all = _pad_rows(mv[s].astype(F32), c).astype(BF16)
        g_all = _sigmoid(mo[s].astype(F32))
        for h in range(N_HEADS):
            kcols = slice(h * DK, (h + 1) * DK)
            vcols = slice(h * DV, (h + 1) * DV)
            qb = qb_all[:, kcols]
            q = qb.astype(F32)
            k = k_all[:, kcols]
            vb = vb_all[:, vcols]
            ic = ig_t[:, SM_I + h:SM_I + h + 1]
            fc = f_t[:, SM_F + h:SM_F + h + 1]
            ms = m_prev[:, h:h + 1]
            a = ic - fc
            a_row = _col_to_row(a, eye)
            cm = jnp.max(jnp.where(causal, jnp.broadcast_to(a_row, (c, c)), NEG), axis=1, keepdims=True)
            m = fc + jnp.maximum(ms, cm)
            w_inter = jnp.exp(fc + ms - m)
            dmat = jnp.exp(jnp.where(causal, (fc - m) + a_row, NEG))
            qk = lax.dot_general(qb, k.astype(BF16), nt_dims, preferred_element_type=F32) * dmat
            cs = c_ref[s, h]
            ns = n_ref[s, h:h + 1, :]
            num = (jnp.dot(qk.astype(BF16), vb, preferred_element_type=F32)
                   + w_inter * jnp.dot(qb, cs.astype(BF16), preferred_element_type=F32))
            den = jnp.sum(qk, axis=1, keepdims=True) + w_inter * jnp.sum(q * ns, axis=1, keepdims=True)
            hh = num / jnp.maximum(jnp.abs(den), jnp.exp(-m))
            m_last = m[c - 1:c, :]
            f_last = fc[c - 1:c, :]
            w_state = jnp.exp(ic + f_last - fc - m_last)
            carry = jnp.exp(f_last + ms - m_last)
            kw = k * w_state
            c_ref[s, h] = carry * cs + lax.dot_general(kw.astype(BF16), vb, tn_dims, preferred_element_type=F32)
            n_ref[s, h:h + 1, :] = carry * ns + jnp.sum(kw, axis=0, keepdims=True)
            m_new_row = jnp.where(lane == h, m_last, m_new_row)
            o_ref[s, :, vcols] = (_rms(hh[:tc], gn[0]) * g_all[:, vcols]).astype(o_ref.dtype)
        m_ref[s] = m_new_row


def _mlstm_mixer(p, small, bias, gn, state, layer, bn, t, bb, out_dtype):
    tc = min(CHUNK, t)
    nch = t // tc
    assert tc * nch == t
    rspec = lambda w: pl.BlockSpec((bb, tc, w), lambda i, j: (i, j, 0))
    r3 = lambda a: a.reshape(bn, t, a.shape[-1])
    in_specs = [rspec(512), rspec(512), rspec(1024), rspec(1024), rspec(SMALL_W),
                _const_spec((1, 1, SMALL_W), lambda i, j: (layer, 0, 0)),
                _const_spec((1, 1, DV), lambda i, j: (layer, 0, 0))]
    args = [r3(p["mq"]), r3(p["mk"]), r3(p["mv"]), r3(p["mo"]), r3(small), bias, gn]
    cspec = pl.BlockSpec((bb, N_HEADS, DK, DV), lambda i, j: (i, 0, 0, 0))
    nspec = pl.BlockSpec((bb, N_HEADS, DK), lambda i, j: (i, 0, 0))
    mspec = pl.BlockSpec((bb, 1, SMALL_W), lambda i, j: (i, 0, 0))
    if state is not None:
        c0, n0, m0 = state
        in_specs += [pl.BlockSpec((None, bb, N_HEADS, DK, DV), lambda i, j: (layer, i, 0, 0, 0)),
                     pl.BlockSpec((None, bb, N_HEADS, DK), lambda i, j: (layer, i, 0, 0)),
                     pl.BlockSpec((None, bb, 1, SMALL_W), lambda i, j: (layer, i, 0, 0))]
        args += [c0, n0, m0]
    o, c_new, n_new, m_new = pl.pallas_call(
        functools.partial(_mlstm_kernel, bb=bb, tc=tc, has_state=state is not None),
        grid=(bn // bb, nch),
        in_specs=in_specs,
        out_specs=[rspec(1024), cspec, nspec, mspec],
        out_shape=[jax.ShapeDtypeStruct((bn, t, 1024), out_dtype),
                   jax.ShapeDtypeStruct((bn, N_HEADS, DK, DV), F32),
                   jax.ShapeDtypeStruct((bn, N_HEADS, DK), F32),
                   jax.ShapeDtypeStruct((bn, 1, SMALL_W), F32)],
        compiler_params=_params(("arbitrary", "arbitrary")),
        name="mlstm_mixer",
    )(*args)
    return o.reshape(bn * t, 1024), c_new, n_new, m_new


def _rglru_kernel(*refs, nb, tc, has_state):
    if has_state:
        rx, cb, h0, cw, cbias, wa, ba, wx, bx, lam, hs_ref, hf_ref, nb_ref, xpad, a_s, u_s = refs
    else:
        rx, cw, cbias, wa, ba, wx, bx, lam, hs_ref, hf_ref, nb_ref, xpad, a_s, u_s = refs
    hist = (CONV_W - 1) * nb
    rows = tc * nb
    w = rx.shape[1]
    bw = w // RG_BLOCKS

    @pl.when(pl.program_id(0) == 0)
    def _():
        if has_state:
            xpad[0:hist, :] = cb[0]
            hf_ref[...] = h0[0]
        else:
            xpad[0:hist, :] = jnp.zeros((hist, w), F32)
            hf_ref[...] = jnp.zeros_like(hf_ref)

    xpad[hist:hist + rows, :] = rx[...].astype(F32)
    xc = cbias[0]
    for j in range(CONV_W):
        xc = xc + xpad[j * nb:j * nb + rows, :] * cw[0, j:j + 1, :]
    sp = _softplus(-lam[0])
    for n in range(RG_BLOCKS):
        cols = slice(n * bw, (n + 1) * bw)
        xn = xc[:, cols]
        xg = xn.astype(BF16)
        r = _sigmoid(jnp.dot(xg, wa[0, n], preferred_element_type=F32) + ba[0, :, cols])
        i = _sigmoid(jnp.dot(xg, wx[0, n], preferred_element_type=F32) + bx[0, :, cols])
        log_a = -RG_C * r * sp[:, cols]
        a_s[:, cols] = jnp.exp(log_a)
        u_s[:, cols] = jnp.sqrt(1.0 - jnp.exp(2.0 * log_a)) * (i * xn)

    def body(t, h):
        o = pl.multiple_of(t * nb, nb)
        h = a_s[pl.ds(o, nb), :] * h + u_s[pl.ds(o, nb), :]
        hs_ref[pl.ds(o, nb), :] = h.astype(hs_ref.dtype)
        return h

    hf_ref[...] = lax.fori_loop(0, tc, body, hf_ref[...])
    tail = xpad[rows:rows + hist, :]
    xpad[0:hist, :] = tail
    nb_ref[...] = tail


def _rglru(rx, conv_buf, h0, wts, layer, nb, t, tc, out_dtype):
    w = rx.shape[1]
    rows = tc * nb
    hist = (CONV_W - 1) * nb
    bw = w // RG_BLOCKS
    lsp = lambda shape: _const_spec((1,) + shape, lambda j: (layer,) + (0,) * len(shape))
    in_specs = [pl.BlockSpec((rows, w), lambda j: (j, 0))]
    args = [rx]
    if conv_buf is not None:
        in_specs += [lsp((hist, w)), lsp((nb, w))]
        args += [conv_buf, h0]
    in_specs += [lsp((CONV_W, w)), lsp((1, w)), lsp((RG_BLOCKS, bw, bw)), lsp((1, w)),
                 lsp((RG_BLOCKS, bw, bw)), lsp((1, w)), lsp((1, w))]
    args += [wts["conv_w"], wts["conv_b"], wts["w_rg_a"], wts["b_rg_a"], wts["w_rg_x"], wts["b_rg_x"],
             wts["rg_lambda"]]
    return pl.pallas_call(
        functools.partial(_rglru_kernel, nb=nb, tc=tc, has_state=conv_buf is not None),
        grid=(t // tc,),
        in_specs=in_specs,
        out_specs=[pl.BlockSpec((rows, w), lambda j: (j, 0)),
                   pl.BlockSpec((nb, w), lambda j: (0, 0)),
                   pl.BlockSpec((hist, w), lambda j: (0, 0))],
        out_shape=[jax.ShapeDtypeStruct((t * nb, w), out_dtype),
                   jax.ShapeDtypeStruct((nb, w), F32),
                   jax.ShapeDtypeStruct((hist, w), F32)],
        scratch_shapes=[pltpu.VMEM((rows + hist, w), F32), pltpu.VMEM((rows, w), F32), pltpu.VMEM((rows, w), F32)],
        compiler_params=_params(("arbitrary",)),
        name="rglru",
    )(*args)


def _merge_kernel(og, om, orr, mg, x_ref, mod_ref, g2, wbg, wbm, wbr, wo, y_ref, h2_ref):
    bb, tb, d = x_ref.shape
    gates = mg[...].astype(F32)
    merged = (_sigmoid(gates[:, 0:d]) * jnp.dot(og[...].astype(BF16), wbg[0], preferred_element_type=F32)
              + _sigmoid(gates[:, d:2 * d]) * jnp.dot(om[...].astype(BF16), wbm[0], preferred_element_type=F32)
              + _sigmoid(gates[:, 2 * d:3 * d]) * jnp.dot(orr[...].astype(BF16), wbr[0], preferred_element_type=F32))
    z = jnp.dot(merged.astype(BF16), wo[0], preferred_element_type=F32)
    mod = mod_ref[...]
    y = x_ref[...] + mod[:, :, 2 * d:3 * d] * z.reshape(bb, tb, d)
    y_ref[...] = y
    h2 = _rms(y, g2[...]) * (1.0 + mod[:, :, 4 * d:5 * d]) + mod[:, :, 3 * d:4 * d]
    h2_ref[...] = h2.reshape(bb * tb, d).astype(BF16)


def _merge(og, om, orr, mg, x, mod, g2, wts, layer, bb, tb, rg_time_major):
    bn, t, d = x.shape
    nt = t // tb
    rows = bb * tb
    rspec = lambda w: pl.BlockSpec((rows, w), lambda i, j: (i * nt + j, 0))
    if rg_time_major:
        assert bb == 1
        rg_spec = pl.BlockSpec((tb, d), lambda i, j: (j, i))
    else:
        rg_spec = rspec(d)
    wspec = _const_spec((1, d, d), lambda i, j: (layer, 0, 0))
    return pl.pallas_call(
        _merge_kernel,
        grid=(bn // bb, nt),
        in_specs=[rspec(d), rspec(d), rg_spec, rspec(3 * d),
                  pl.BlockSpec((bb, tb, d), lambda i, j: (i, j, 0)),
                  pl.BlockSpec((bb, 1, 6 * d), lambda i, j: (i, 0, 0)),
                  _const_spec((1, 1, d), lambda i, j: (layer, 0, 0)),
                  wspec, wspec, wspec, wspec],
        out_specs=[pl.BlockSpec((bb, tb, d), lambda i, j: (i, j, 0)), rspec(d)],
        out_shape=[jax.ShapeDtypeStruct((bn, t, d), F32), jax.ShapeDtypeStruct((bn * t, d), BF16)],
        compiler_params=_params(("arbitrary", "arbitrary")),
        name="merge",
    )(og, om, orr, mg, x, mod, g2, wts["w_br_gla"], wts["w_br_ml"], wts["w_br_rg"], wts["w_out"])


def _sort16_pairs():
    n, pairs, p = 16, [], 1
    while p < n:
        k = p
        while k >= 1:
            for j in range(k % p, n - k, 2 * k):
                for i in range(min(k, n - j - k)):
                    if (i + j) // (2 * p) == (i + j + k) // (2 * p):
                        pairs.append((i + j, i + j + k))
            k //= 2
        p *= 2
    return pairs


_SORT16 = _sort16_pairs()
_BITONIC16 = [(i, i + dd) for dd in (8, 4, 2, 1) for i in range(16) if (i & dd) == 0]
_CANDS = [(a, b) for a in range(PEER_TOPK) for b in range(PEER_TOPK) if (a + 1) * (b + 1) <= PEER_TOPK]


def _top16_sorted(st):
    xs = [st[g * 8:(g + 1) * 8, :] for g in range(16)]
    for i, j in _SORT16:
        hi, lo = jnp.maximum(xs[i], xs[j]), jnp.minimum(xs[i], xs[j])
        xs[i], xs[j] = hi, lo
    for shift in (4, 2, 1):
        ys = [jnp.maximum(xs[r], pltpu.roll(xs[15 - r], shift, 0)) for r in range(16)]
        for i, j in _BITONIC16:
            hi, lo = jnp.maximum(ys[i], ys[j]), jnp.minimum(ys[i], ys[j])
            ys[i], ys[j] = hi, lo
        xs = ys
    return xs


def _peer_kernel(h2_ref, y_ref, mod_ref, wq_ref, keys_ref, u_ref, vt_ref, fg_ref, o_ref,
                 s0_ref, s1_ref, cc_ref, e1_ref, th_ref, pta_ref, ptb_ref, gt_ref, acc_ref, *, final, sub):
    bb, tb, d = y_ref.shape
    tm = bb * tb
    ec = u_ref.shape[1]
    cidx = pl.program_id(1)
    nt_dims = (((1,), (1,)), ((), ()))

    @pl.when(cidx == 0)
    def _():
        q = jnp.dot(h2_ref[...], wq_ref[0], preferred_element_type=F32)
        sub8 = lax.broadcasted_iota(jnp.int32, (8, tm), 0)
        tops = []
        for p in range(2):
            stack = [jnp.zeros((8, tm), F32) for _ in range(PEER_TOPK)]
            for h in range(PEER_HEADS):
                col = (h * 2 + p) * PEER_NKEYS
                qs = q[:, col:col + PEER_NKEYS].astype(BF16)
                st = lax.dot_general(keys_ref[0, h, p], qs, nt_dims, preferred_element_type=F32)
                if p == 0:
                    for l in range(tm // LANES):
                        s0_ref[h, l] = st[:, l * LANES:(l + 1) * LANES]
                else:
                    s1_ref[h] = st
                srt = _top16_sorted(st)
                stack = [jnp.where(sub8 == h, srt[r], stack[r]) for r in range(PEER_TOPK)]
            tops.append(stack)
        cands = [tops[0][a] + tops[1][b] for a, b in _CANDS]
        work = list(cands)
        theta = jnp.full((8, tm), NEG, F32)
        cnt = jnp.zeros((8, tm), F32)
        for _ in range(PEER_TOPK):
            mx = functools.reduce(jnp.maximum, work)
            eq = [w == mx for w in work]
            theta = jnp.where(cnt < PEER_TOPK, mx, theta)
            cnt = cnt + functools.reduce(jnp.add, [e.astype(F32) for e in eq])
            work = [jnp.where(e, NEG, w) for e, w in zip(eq, work)]
        top = tops[0][0] + tops[1][0]
        z = functools.reduce(jnp.add, [jnp.where(cd >= theta, jnp.exp(cd - top), 0.0) for cd in cands])
        inv_z = 1.0 / z
        for l in range(tm // LANES):
            lanes = slice(l * LANES, (l + 1) * LANES)
            th_ref[l] = theta[:, lanes]
            for h in range(PEER_HEADS):
                cc_ref[h, l] = jnp.exp(s0_ref[h, l] - tops[0][0][h:h + 1, lanes]) * inv_z[h:h + 1, lanes]
        for h in range(PEER_HEADS):
            e1_ref[h] = jnp.exp(s1_ref[h] - tops[1][0][h:h + 1, :])
        acc_ref[...] = jnp.zeros_like(acc_ref)
        ptb_ref[...] = jnp.zeros_like(ptb_ref)

    groups = ec // PEER_NKEYS
    nsub = PEER_NKEYS // sub
    prev = jnp.maximum(cidx - 1, 0)

    def step(pt_next, pt_cur):
        pt_next[...] = lax.dot_general(u_ref[0], h2_ref[...], nt_dims, preferred_element_type=F32)
        for n in range(groups * nsub):
            i = prev * groups + n // nsub
            r0 = n * sub
            j0 = (n % nsub) * sub
            for l in range(tm // LANES):
                lanes = slice(l * LANES, (l + 1) * LANES)
                pre = pt_cur[r0:r0 + sub, lanes]
                wgt = jnp.zeros((sub, LANES), F32)
                for h in range(PEER_HEADS):
                    zz = s1_ref[h, j0:j0 + sub, lanes] + s0_ref[h, l, pl.ds(i, sub, stride=0), :]
                    wgt = wgt + jnp.where(zz >= th_ref[l, pl.ds(h, sub, stride=0), :],
                                          e1_ref[h, j0:j0 + sub, lanes] * cc_ref[h, l, pl.ds(i, sub, stride=0), :],
                                          0.0)
                act = 0.5 * pre * (1.0 + lax.erf(pre * 0.7071067811865476))
                gt_ref[r0:r0 + sub, lanes] = (act * wgt).astype(BF16)
        acc_ref[...] += jnp.dot(vt_ref[0], gt_ref[...], preferred_element_type=F32)

    @pl.when(cidx % 2 == 0)
    def _():
        step(pta_ref, ptb_ref)

    @pl.when(cidx % 2 == 1)
    def _():
        step(ptb_ref, pta_ref)

    @pl.when(cidx == pl.num_programs(1) - 1)
    def _():
        peer = acc_ref[...].T.reshape(bb, tb, d)
        y = y_ref[...] + mod_ref[...][:, :, 5 * d:6 * d] * peer
        if final:
            y = _rms(y, fg_ref[...])
        o_ref[...] = y


def _peer(h2, y, mod, wts, final_g, layer, bb, tb, ec, final):
    bn, t, d = y.shape
    nt = t // tb
    tm = bb * tb
    n_exp = wts["peer_u"].shape[1]
    nq = wts["w_peer_q"].shape[-1]
    sub = 32
    assert tm % LANES == 0
    tile = lambda: pltpu.VMEM((PEER_HEADS, PEER_NKEYS, tm), F32)
    row_tile = lambda: pltpu.VMEM((PEER_HEADS, tm // LANES, PEER_NKEYS, LANES), F32)
    nc = n_exp // ec
    return pl.pallas_call(
        functools.partial(_peer_kernel, final=final, sub=sub),
        grid=(bn * t // tm, nc + 1),
        in_specs=[pl.BlockSpec((tm, d), lambda i, c: (i, 0)),
                  pl.BlockSpec((bb, tb, d), lambda i, c: (i // nt, i % nt, 0)),
                  pl.BlockSpec((bb, 1, 6 * d), lambda i, c: (i // nt, 0, 0)),
                  _const_spec((1, d, nq), lambda i, c: (layer, 0, 0)),
                  _const_spec((1, PEER_HEADS, 2, PEER_NKEYS, PEER_NKEYS), lambda i, c: (layer, 0, 0, 0, 0)),
                  pl.BlockSpec((1, ec, d), lambda i, c: (layer, jnp.minimum(c, nc - 1), 0)),
                  pl.BlockSpec((1, d, ec), lambda i, c: (layer, 0, jnp.maximum(c - 1, 0))),
                  _const_spec((1, 1, d), lambda i, c: (0, 0, 0))],
        out_specs=pl.BlockSpec((bb, tb, d), lambda i, c: (i // nt, i % nt, 0)),
        out_shape=jax.ShapeDtypeStruct((bn, t, d), F32),
        scratch_shapes=[row_tile(), tile(), row_tile(), tile(),
                        pltpu.VMEM((tm // LANES, PEER_HEADS, LANES), F32),
                        pltpu.VMEM((ec, tm), F32),
                        pltpu.VMEM((ec, tm), F32),
                        pltpu.VMEM((ec, tm), BF16),
                        pltpu.VMEM((d, tm), F32)],
        compiler_params=_params(("arbitrary", "arbitrary")),
        name="peer",
    )(h2, y, mod, wts["w_peer_q"], wts["peer_keys"], wts["peer_u"], wts["peer_vt"], final_g)


def _prepare_weights(w_in, w_gla_f2, b_gla_f, gla_norm_g, b_ml_i, b_ml_f, ml_norm_g, conv_w, conv_b, w_rg_a, b_rg_a,
                     w_rg_x, b_rg_x, rg_lambda, w_br_gla, w_br_ml, w_br_rg, w_out, w_peer_q, peer_keys, peer_u, peer_v,
                     norm1_g, norm2_g):
    n_layers, d, _ = w_in.shape
    offs, o = {}, 0
    for name, w in (("gq", 512), ("gk", 512), ("gv", 1024), ("gg", 1024), ("gf", GLA_RANK),
                    ("mq", 512), ("mk", 512), ("mv", 1024), ("mo", 1024), ("mi", N_HEADS), ("mf", N_HEADS),
                    ("rx", 1024), ("mg", 3072)):
        offs[name] = (o, o + w)
        o += w
    assert o == w_in.shape[-1]
    cut = lambda name: w_in[:, :, offs[name][0]:offs[name][1]]
    w_main = jnp.concatenate([cut(n) for n, _ in IN_PIECES], axis=-1).astype(BF16)
    n_small = GLA_RANK + 2 * N_HEADS
    w_small = jnp.concatenate([cut("gf"), cut("mi"), cut("mf"),
                               jnp.zeros((n_layers, d, SMALL_W - n_small), F32)], axis=-1).astype(BF16)
    wf2 = jnp.concatenate([w_gla_f2, jnp.zeros((n_layers, SMALL_W - GLA_RANK, w_gla_f2.shape[-1]), F32)],
                          axis=1).astype(BF16)
    bias_small = jnp.concatenate([jnp.zeros((n_layers, GLA_RANK), F32), b_ml_i, b_ml_f,
                                  jnp.zeros((n_layers, SMALL_W - n_small), F32)], axis=-1)
    row = lambda a: a.reshape(n_layers, 1, a.shape[-1])
    return dict(
        w_main=w_main, w_small=w_small, wf2=wf2, b_gla_f=row(b_gla_f), gla_norm_g=row(gla_norm_g),
        bias_small=row(bias_small), ml_norm_g=row(ml_norm_g),
        conv_w=conv_w, conv_b=row(conv_b), w_rg_a=w_rg_a.astype(BF16), b_rg_a=row(b_rg_a),
        w_rg_x=w_rg_x.astype(BF16), b_rg_x=row(b_rg_x), rg_lambda=row(rg_lambda),
        w_br_gla=w_br_gla.astype(BF16), w_br_ml=w_br_ml.astype(BF16), w_br_rg=w_br_rg.astype(BF16),
        w_out=w_out.astype(BF16), w_peer_q=w_peer_q.astype(BF16), peer_keys=peer_keys.astype(BF16),
        peer_u=peer_u.astype(BF16), peer_vt=jnp.swapaxes(peer_v, 1, 2).astype(BF16),
        norm1_g=row(norm1_g), norm2_g=row(norm2_g))


def _layer(x, mod, state, wts, final_g, layer, final, cfg):
    bn, t, d = x.shape
    fresh = state is None
    act_dtype = BF16 if fresh else F32
    outs = _in_projection(x, mod, wts["norm1_g"], wts["w_main"], wts["w_small"], layer,
                          cfg["bb"], cfg["tb"], act_dtype, rx_time_major=fresh)
    p = {name: o for (name, _), o in zip(IN_PIECES, outs[:-1])}
    small = outs[-1]
    if fresh:
        s_gla = ml_state = None
        rx = p["rx"].reshape(t * bn, d)
        conv_buf = h0 = None
    else:
        s_gla, c0, n0, m0, h0, conv_buf = state
        ml_state = (c0, n0, m0)
        rx = jnp.swapaxes(p["rx"].reshape(bn, t, d), 0, 1).reshape(t * bn, d)
    o_gla, s_gla_new = _gla_mixer(p, small, wts["wf2"], wts["b_gla_f"], wts["gla_norm_g"], s_gla, layer,
                                  bn, t, cfg["gla_bb"], act_dtype)
    o_ml, c_new, n_new, m_new = _mlstm_mixer(p, small, wts["bias_small"], wts["ml_norm_g"], ml_state, layer,
                                             bn, t, cfg["ml_bb"], act_dtype)
    hs, h_fin, buf_new = _rglru(rx, conv_buf, h0, wts, layer, bn, t, cfg["rg_tc"], act_dtype)
    if fresh:
        o_rg = hs.reshape(t, bn * d)
    else:
        o_rg = jnp.swapaxes(hs.reshape(t, bn, d), 0, 1).reshape(bn * t, d)
    y, h2 = _merge(o_gla, o_ml, o_rg, p["mg"], x, mod, wts["norm2_g"], wts, layer, cfg["bb"], cfg["tb"],
                   rg_time_major=fresh)
    out = _peer(h2, y, mod, wts, final_g, layer, cfg["peer_bb"], cfg["peer_tb"], cfg["peer_ec"], final)
    new_state = (s_gla_new, c_new, n_new, m_new[:, 0, :N_HEADS], h_fin,
                 jnp.swapaxes(buf_new.reshape(CONV_W - 1, bn, d), 0, 1))
    return out, new_state


def kernel(x_prompt, x_sample, state_gla, state_mlstm_c, state_mlstm_n, state_mlstm_m, state_rglru_h, state_rglru_conv, c_prompt, c_sample, w_ada, b_ada, norm1_g, norm2_g, w_in, w_gla_f2, b_gla_f, gla_norm_g, b_ml_i, b_ml_f, ml_norm_g, conv_w, conv_b, w_rg_a, b_rg_a, w_rg_x, b_rg_x, rg_lambda, w_br_gla, w_br_ml, w_br_rg, w_out, w_peer_q, peer_keys, peer_u, peer_v, final_norm_g):
    n_layers = w_in.shape[0]
    bp, tp, d = x_prompt.shape
    bs, ts, _ = x_sample.shape
    wts = _prepare_weights(w_in, w_gla_f2, b_gla_f, gla_norm_g, b_ml_i, b_ml_f, ml_norm_g, conv_w, conv_b, w_rg_a,
                           b_rg_a, w_rg_x, b_rg_x, rg_lambda, w_br_gla, w_br_ml, w_br_rg, w_out, w_peer_q,
                           peer_keys, peer_u, peer_v, norm1_g, norm2_g)
    final_g = final_norm_g.reshape(1, 1, d)
    mod_all = _modulation(jnp.concatenate([c_prompt, c_sample], axis=0), w_ada, b_ada)
    m0_pad = jnp.pad(state_mlstm_m, ((0, 0), (0, 0), (0, SMALL_W - N_HEADS))).reshape(n_layers, bs, 1, SMALL_W)
    conv_tm = jnp.swapaxes(state_rglru_conv, 1, 2).reshape(n_layers, (CONV_W - 1) * bs, d)
    sample_state = (state_gla, state_mlstm_c, state_mlstm_n, m0_pad, state_rglru_h, conv_tm)

    cfg_p = dict(bb=1, tb=min(256, tp), gla_bb=min(4, bp), ml_bb=min(2, bp), rg_tc=min(128, tp), peer_bb=1,
                 peer_tb=min(512, tp), peer_ec=1024)
    cfg_s = dict(bb=min(16, bs), tb=ts, gla_bb=min(4, bs), ml_bb=min(4, bs), rg_tc=ts, peer_bb=min(512 // ts, bs),
                 peer_tb=ts, peer_ec=1024)
    yp, ys = x_prompt, x_sample
    new_p, new_s = [], []
    for l in range(n_layers):
        final = l == n_layers - 1
        mod_p = mod_all[l, :bp].reshape(bp, 1, 6 * d)
        mod_s = mod_all[l, bp:].reshape(bs, 1, 6 * d)
        yp, st_p = _layer(yp, mod_p, None, wts, final_g, l, final, cfg_p)
        ys, st_s = _layer(ys, mod_s, sample_state, wts, final_g, l, final, cfg_s)
        new_p.append(st_p)
        new_s.append(st_s)
    stacked_p = [jnp.stack(z) for z in zip(*new_p)]
    stacked_s = [jnp.stack(z) for z in zip(*new_s)]
    return (yp, ys, *stacked_p, *stacked_s)
```

```python
import functools

import jax
import jax.numpy as jnp
from jax import lax
from jax.experimental import pallas as pl
from jax.experimental.pallas import tpu as pltpu

F32 = jnp.float32
BF16 = jnp.bfloat16
HIGHEST = lax.Precision.HIGHEST

EPS = 1e-6
N_HEADS = 4
DK = 128
DV = 256
GLA_RANK = 16
GLA_GATE_TEMP = 16.0
CHUNK = 64
RG_BLOCKS = 8
RG_C = 8.0
CONV_W = 4
PEER_HEADS = 8
PEER_NKEYS = 128
PEER_TOPK = 16
NEG = -1e30
LANES = 128

VMEM_LIMIT_BYTES = 56 * 1024 * 1024

IN_PIECES = (("gq", 512), ("gk", 512), ("gv", 1024), ("gg", 1024),
             ("mq", 512), ("mk", 512), ("mv", 1024), ("mo", 1024),
             ("rx", 1024), ("mg", 3072))
SMALL_W = 128
SM_I = GLA_RANK
SM_F = GLA_RANK + N_HEADS


def _params(sem):
    return pltpu.CompilerParams(dimension_semantics=sem, vmem_limit_bytes=VMEM_LIMIT_BYTES)


def _const_spec(shape, index_map):
    return pl.BlockSpec(shape, index_map, pipeline_mode=pl.Buffered(1))


def _log_sigmoid(x):
    return jnp.minimum(x, 0.0) - jnp.log1p(jnp.exp(-jnp.abs(x)))


def _softplus(x):
    return jnp.maximum(x, 0.0) + jnp.log1p(jnp.exp(-jnp.abs(x)))


def _sigmoid(x):
    return 1.0 / (1.0 + jnp.exp(-x))


def _rms(x, g):
    return x * lax.rsqrt(jnp.mean(x * x, axis=-1, keepdims=True) + EPS) * g


def _pad_rows(a, rows, value=0.0):
    if a.shape[0] == rows:
        return a
    return jnp.concatenate([a, jnp.full((rows - a.shape[0], a.shape[1]), value, a.dtype)], axis=0)


def _row_to_col(row, eye):
    n = row.shape[1]
    return jnp.sum(jnp.where(eye, jnp.broadcast_to(row, (n, n)), 0.0), axis=1, keepdims=True)


def _col_to_row(col, eye):
    n = col.shape[0]
    return jnp.sum(jnp.where(eye, jnp.broadcast_to(col, (n, n)), 0.0), axis=0, keepdims=True)


def _mod_kernel(c_ref, w_ref, b_ref, o_ref):
    c = c_ref[...]
    s = (c * _sigmoid(c)).astype(BF16)
    o_ref[0] = jnp.dot(s, w_ref[0].astype(BF16), preferred_element_type=F32) + b_ref[0]


def _modulation(c_all, w_ada, b_ada):
    n_layers, d, n6 = w_ada.shape
    rows = c_all.shape[0]
    tn = 1536
    return pl.pallas_call(
        _mod_kernel,
        grid=(n_layers, n6 // tn),
        in_specs=[pl.BlockSpec((rows, d), lambda l, n: (0, 0)),
                  pl.BlockSpec((1, d, tn), lambda l, n: (l, 0, n)),
                  pl.BlockSpec((1, 1, tn), lambda l, n: (l, 0, n))],
        out_specs=pl.BlockSpec((1, rows, tn), lambda l, n: (l, 0, n)),
        out_shape=jax.ShapeDtypeStruct((n_layers, rows, n6), F32),
        compiler_params=_params(("arbitrary", "arbitrary")),
        name="adaln_mod",
    )(c_all, w_ada, b_ada.reshape(n_layers, 1, n6))


def _inproj_kernel(x_ref, mod_ref, g_ref, wm_ref, ws_ref, *out_refs):
    bb, tb, d = x_ref.shape
    mod = mod_ref[...]
    h = _rms(x_ref[...], g_ref[...]) * (1.0 + mod[:, :, d:2 * d]) + mod[:, :, 0:d]
    hb = h.reshape(bb * tb, d).astype(BF16)
    off = 0
    for r, (_, w) in zip(out_refs[:-1], IN_PIECES):
        r[...] = jnp.dot(hb, wm_ref[0, :, off:off + w], preferred_element_type=F32).astype(r.dtype)
        off += w
    out_refs[-1][...] = jnp.dot(hb, ws_ref[0], preferred_element_type=F32)


def _in_projection(x, mod, g1, w_main, w_small, layer, bb, tb, out_dtype, rx_time_major):
    bn, t, d = x.shape
    nt = t // tb
    rows = bb * tb
    n_main = w_main.shape[-1]
    out_shapes, out_specs = [], []
    for name, w in IN_PIECES:
        if name == "rx" and rx_time_major:
            out_shapes.append(jax.ShapeDtypeStruct((t, bn * w), out_dtype))
            out_specs.append(pl.BlockSpec((tb, w), lambda i, j: (j, i)))
        else:
            out_shapes.append(jax.ShapeDtypeStruct((bn * t, w), out_dtype))
            out_specs.append(pl.BlockSpec((rows, w), lambda i, j: (i * nt + j, 0)))
    out_shapes.append(jax.ShapeDtypeStruct((bn * t, SMALL_W), F32))
    out_specs.append(pl.BlockSpec((rows, SMALL_W), lambda i, j: (i * nt + j, 0)))
    if rx_time_major:
        assert bb == 1
    return pl.pallas_call(
        _inproj_kernel,
        grid=(bn // bb, nt),
        in_specs=[pl.BlockSpec((bb, tb, d), lambda i, j: (i, j, 0)),
                  pl.BlockSpec((bb, 1, 6 * d), lambda i, j: (i, 0, 0)),
                  _const_spec((1, 1, d), lambda i, j: (layer, 0, 0)),
                  _const_spec((1, d, n_main), lambda i, j: (layer, 0, 0)),
                  _const_spec((1, d, SMALL_W), lambda i, j: (layer, 0, 0))],
        out_specs=out_specs,
        out_shape=out_shapes,
        compiler_params=_params(("arbitrary", "arbitrary")),
        name="in_proj",
    )(x, mod, g1, w_main, w_small)


def _gla_kernel(*refs, bb, tc, has_state):
    if has_state:
        gq, gk, gv, gg, sm, wf2, bf, gn, s0, o_ref, s_ref = refs
    else:
        gq, gk, gv, gg, sm, wf2, bf, gn, o_ref, s_ref = refs
        s0 = None
    c = CHUNK

    @pl.when(pl.program_id(1) == 0)
    def _():
        if has_state:
            s_ref[...] = s0[...]
        else:
            s_ref[...] = jnp.zeros_like(s_ref)

    ri = lax.broadcasted_iota(jnp.int32, (c, c), 0)
    ci = lax.broadcasted_iota(jnp.int32, (c, c), 1)
    causal = ri >= ci
    tril = causal.astype(F32)
    hk = N_HEADS * DK
    valid = lax.broadcasted_iota(jnp.int32, (c, hk), 0) < tc
    eye = (lax.broadcasted_iota(jnp.int32, (DK, DK), 0) == lax.broadcasted_iota(jnp.int32, (DK, DK), 1))
    nt_dims = (((1,), (1,)), ((), ()))
    tn_dims = (((0,), (0,)), ((), ()))
    for s in range(bb):
        smb = _pad_rows(sm[s], c).astype(BF16)
        x = jnp.dot(smb, wf2[0], preferred_element_type=F32) + bf[0]
        logf = jnp.where(valid, _log_sigmoid(x) * (1.0 / GLA_GATE_TEMP), 0.0)
        b = jnp.dot(tril, logf, precision=HIGHEST, preferred_element_type=F32)
        b_last = b[c - 1:c, :]
        q = _pad_rows(gq[s].astype(F32), c) * (DK ** -0.5)
        k = _pad_rows(gk[s].astype(F32), c)
        qd = (q * jnp.exp(b)).astype(BF16)
        kd = (k * jnp.exp(-b)).astype(BF16)
        kend = (k * jnp.exp(b_last - b)).astype(BF16)
        vb = _pad_rows(gv[s].astype(F32), c).astype(BF16)
        g = gg[s].astype(F32)
        g = g * _sigmoid(g)
        for h in range(N_HEADS):
            kcols = slice(h * DK, (h + 1) * DK)
            vcols = slice(h * DV, (h + 1) * DV)
            att = lax.dot_general(qd[:, kcols], kd[:, kcols], nt_dims, preferred_element_type=F32)
            att = jnp.where(causal, att, 0.0).astype(BF16)
            st = s_ref[s, h]
            o = (jnp.dot(att, vb[:, vcols], preferred_element_type=F32)
                 + jnp.dot(qd[:, kcols], st.astype(BF16), preferred_element_type=F32))
            dec = jnp.exp(_row_to_col(b_last[:, kcols], eye))
            s_ref[s, h] = dec * st + lax.dot_general(kend[:, kcols], vb[:, vcols], tn_dims,
                                                     preferred_element_type=F32)
            o_ref[s, :, vcols] = (_rms(o[:tc], gn[0]) * g[:, vcols]).astype(o_ref.dtype)


def _gla_mixer(p, small, wf2, bfv, gn, s0, layer, bn, t, bb, out_dtype):
    tc = min(CHUNK, t)
    nch = t // tc
    assert tc * nch == t
    rspec = lambda w: pl.BlockSpec((bb, tc, w), lambda i, j: (i, j, 0))
    r3 = lambda a: a.reshape(bn, t, a.shape[-1])
    in_specs = [rspec(512), rspec(512), rspec(1024), rspec(1024), rspec(SMALL_W),
                _const_spec((1, SMALL_W, 512), lambda i, j: (layer, 0, 0)),
                _const_spec((1, 1, 512), lambda i, j: (layer, 0, 0)),
                _const_spec((1, 1, DV), lambda i, j: (layer, 0, 0))]
    args = [r3(p["gq"]), r3(p["gk"]), r3(p["gv"]), r3(p["gg"]), r3(small), wf2, bfv, gn]
    sspec = pl.BlockSpec((bb, N_HEADS, DK, DV), lambda i, j: (i, 0, 0, 0))
    if s0 is not None:
        in_specs.append(pl.BlockSpec((None, bb, N_HEADS, DK, DV), lambda i, j: (layer, i, 0, 0, 0)))
        args.append(s0)
    o, s_new = pl.pallas_call(
        functools.partial(_gla_kernel, bb=bb, tc=tc, has_state=s0 is not None),
        grid=(bn // bb, nch),
        in_specs=in_specs,
        out_specs=[rspec(1024), sspec],
        out_shape=[jax.ShapeDtypeStruct((bn, t, 1024), out_dtype),
                   jax.ShapeDtypeStruct((bn, N_HEADS, DK, DV), F32)],
        compiler_params=_params(("arbitrary", "arbitrary")),
        name="gla_mixer",
    )(*args)
    return o.reshape(bn * t, 1024), s_new


def _mlstm_kernel(*refs, bb, tc, has_state):
    if has_state:
        mq, mk, mv, mo, sm, bias, gn, c0, n0, m0, o_ref, c_ref, n_ref, m_ref = refs
    else:
        mq, mk, mv, mo, sm, bias, gn, o_ref, c_ref, n_ref, m_ref = refs
    c = CHUNK

    @pl.when(pl.program_id(1) == 0)
    def _():
        if has_state:
            c_ref[...] = c0[...]
            n_ref[...] = n0[...]
            m_ref[...] = m0[...]
        else:
            c_ref[...] = jnp.zeros_like(c_ref)
            n_ref[...] = jnp.zeros_like(n_ref)
            m_ref[...] = jnp.zeros_like(m_ref)

    ri = lax.broadcasted_iota(jnp.int32, (c, c), 0)
    ci = lax.broadcasted_iota(jnp.int32, (c, c), 1)
    causal = ri >= ci
    eye = ri == ci
    tril = causal.astype(F32)
    valid = lax.broadcasted_iota(jnp.int32, (c, SMALL_W), 0) < tc
    lane = lax.broadcasted_iota(jnp.int32, (1, SMALL_W), 1)
    nt_dims = (((1,), (1,)), ((), ()))
    tn_dims = (((0,), (0,)), ((), ()))
    for s in range(bb):
        t1 = _pad_rows(sm[s], c) + bias[0]
        ig_t = jnp.where(valid, t1, NEG)
        lf_t = jnp.where(valid, _log_sigmoid(t1), 0.0)
        f_t = jnp.dot(tril, lf_t, precision=HIGHEST, preferred_element_type=F32)
        m_prev = m_ref[s]
        m_new_row = m_prev
        qb_all = _pad_rows(mq[s].astype(F32), c).astype(BF16)
        k_all = _pad_rows(mk[s].astype(F32), c) * (DK ** -0.5)
        vb_all = _pad_rows(mv[s].astype(F32), c).astype(BF16)
        g_all = _sigmoid(mo[s].astype(F32))
        for h in range(N_HEADS):
            kcols = slice(h * DK, (h + 1) * DK)
            vcols = slice(h * DV, (h + 1) * DV)
            qb = qb_all[:, kcols]
            q = qb.astype(F32)
            k = k_all[:, kcols]
            vb = vb_all[:, vcols]
            ic = ig_t[:, SM_I + h:SM_I + h + 1]
            fc = f_t[:, SM_F + h:SM_F + h + 1]
            ms = m_prev[:, h:h + 1]
            a = ic - fc
            a_row = _col_to_row(a, eye)
            cm = jnp.max(jnp.where(causal, jnp.broadcast_to(a_row, (c, c)), NEG), axis=1, keepdims=True)
            m = fc + jnp.maximum(ms, cm)
            w_inter = jnp.exp(fc + ms - m)
            dmat = jnp.exp(jnp.where(causal, (fc - m) + a_row, NEG))
            qk = lax.dot_general(qb, k.astype(BF16), nt_dims, preferred_element_type=F32) * dmat
            cs = c_ref[s, h]
            ns = n_ref[s, h:h + 1, :]
            num = (jnp.dot(qk.astype(BF16), vb, preferred_element_type=F32)
                   + w_inter * jnp.dot(qb, cs.astype(BF16), preferred_element_type=F32))
            den = jnp.sum(qk, axis=1, keepdims=True) + w_inter * jnp.sum(q * ns, axis=1, keepdims=True)
            hh = num / jnp.maximum(jnp.abs(den), jnp.exp(-m))
            m_last = m[c - 1:c, :]
            f_last = fc[c - 1:c, :]
            w_state = jnp.exp(ic + f_last - fc - m_last)
            carry = jnp.exp(f_last + ms - m_last)
            kw = k * w_state
            c_ref[s, h] = carry * cs + lax.dot_general(kw.astype(BF16), vb, tn_dims, preferred_element_type=F32)
            n_ref[s, h:h + 1, :] = carry * ns + jnp.sum(kw, axis=0, keepdims=True)
            m_new_row = jnp.where(lane == h, m_last, m_new_row)
            o_ref[s, :, vcols] = (_rms(hh[:tc], gn[0]) * g_all[:, vcols]).astype(o_ref.dtype)
        m_ref[s] = m_new_row


def _mlstm_mixer(p, small, bias, gn, state, layer, bn, t, bb, out_dtype):
    tc = min(CHUNK, t)
    nch = t // tc
    assert tc * nch == t
    rspec = lambda w: pl.BlockSpec((bb, tc, w), lambda i, j: (i, j, 0))
    r3 = lambda a: a.reshape(bn, t, a.shape[-1])
    in_specs = [rspec(512), rspec(512), rspec(1024), rspec(1024), rspec(SMALL_W),
                _const_spec((1, 1, SMALL_W), lambda i, j: (layer, 0, 0)),
                _const_spec((1, 1, DV), lambda i, j: (layer, 0, 0))]
    args = [r3(p["mq"]), r3(p["mk"]), r3(p["mv"]), r3(p["mo"]), r3(small), bias, gn]
    cspec = pl.BlockSpec((bb, N_HEADS, DK, DV), lambda i, j: (i, 0, 0, 0))
    nspec = pl.BlockSpec((bb, N_HEADS, DK), lambda i, j: (i, 0, 0))
    mspec = pl.BlockSpec((bb, 1, SMALL_W), lambda i, j: (i, 0, 0))
    if state is not None:
        c0, n0, m0 = state
        in_specs += [pl.BlockSpec((None, bb, N_HEADS, DK, DV), lambda i, j: (layer, i, 0, 0, 0)),
                     pl.BlockSpec((None, bb, N_HEADS, DK), lambda i, j: (layer, i, 0, 0)),
                     pl.BlockSpec((None, bb, 1, SMALL_W), lambda i, j: (layer, i, 0, 0))]
        args += [c0, n0, m0]
    o, c_new, n_new, m_new = pl.pallas_call(
        functools.partial(_mlstm_kernel, bb=bb, tc=tc, has_state=state is not None),
        grid=(bn // bb, nch),
        in_specs=in_specs,
        out_specs=[rspec(1024), cspec, nspec, mspec],
        out_shape=[jax.ShapeDtypeStruct((bn, t, 1024), out_dtype),
                   jax.ShapeDtypeStruct((bn, N_HEADS, DK, DV), F32),
                   jax.ShapeDtypeStruct((bn, N_HEADS, DK), F32),
                   jax.ShapeDtypeStruct((bn, 1, SMALL_W), F32)],
        compiler_params=_params(("arbitrary", "arbitrary")),
        name="mlstm_mixer",
    )(*args)
    return o.reshape(bn * t, 1024), c_new, n_new, m_new


def _rglru_kernel(*refs, nb, tc, has_state):
    if has_state:
        rx, cb, h0, cw, cbias, wa, ba, wx, bx, lam, hs_ref, hf_ref, nb_ref, xpad, a_s, u_s = refs
    else:
        rx, cw, cbias, wa, ba, wx, bx, lam, hs_ref, hf_ref, nb_ref, xpad, a_s, u_s = refs
    hist = (CONV_W - 1) * nb
    rows = tc * nb
    w = rx.shape[1]
    bw = w // RG_BLOCKS

    @pl.when(pl.program_id(0) == 0)
    def _():
        if has_state:
            xpad[0:hist, :] = cb[0]
            hf_ref[...] = h0[0]
        else:
            xpad[0:hist, :] = jnp.zeros((hist, w), F32)
            hf_ref[...] = jnp.zeros_like(hf_ref)

    xpad[hist:hist + rows, :] = rx[...].astype(F32)
    xc = cbias[0]
    for j in range(CONV_W):
        xc = xc + xpad[j * nb:j * nb + rows, :] * cw[0, j:j + 1, :]
    sp = _softplus(-lam[0])
    for n in range(RG_BLOCKS):
        cols = slice(n * bw, (n + 1) * bw)
        xn = xc[:, cols]
        xg = xn.astype(BF16)
        r = _sigmoid(jnp.dot(xg, wa[0, n], preferred_element_type=F32) + ba[0, :, cols])
        i = _sigmoid(jnp.dot(xg, wx[0, n], preferred_element_type=F32) + bx[0, :, cols])
        log_a = -RG_C * r * sp[:, cols]
        a_s[:, cols] = jnp.exp(log_a)
        u_s[:, cols] = jnp.sqrt(1.0 - jnp.exp(2.0 * log_a)) * (i * xn)

    def body(t, h):
        o = pl.multiple_of(t * nb, nb)
        h = a_s[pl.ds(o, nb), :] * h + u_s[pl.ds(o, nb), :]
        hs_ref[pl.ds(o, nb), :] = h.astype(hs_ref.dtype)
        return h

    hf_ref[...] = lax.fori_loop(0, tc, body, hf_ref[...])
    tail = xpad[rows:rows + hist, :]
    xpad[0:hist, :] = tail
    nb_ref[...] = tail


def _rglru(rx, conv_buf, h0, wts, layer, nb, t, tc, out_dtype):
    w = rx.shape[1]
    rows = tc * nb
    hist = (CONV_W - 1) * nb
    bw = w // RG_BLOCKS
    lsp = lambda shape: _const_spec((1,) + shape, lambda j: (layer,) + (0,) * len(shape))
    in_specs = [pl.BlockSpec((rows, w), lambda j: (j, 0))]
    args = [rx]
    if conv_buf is not None:
        in_specs += [lsp((hist, w)), lsp((nb, w))]
        args += [conv_buf, h0]
    in_specs += [lsp((CONV_W, w)), lsp((1, w)), lsp((RG_BLOCKS, bw, bw)), lsp((1, w)),
                 lsp((RG_BLOCKS, bw, bw)), lsp((1, w)), lsp((1, w))]
    args += [wts["conv_w"], wts["conv_b"], wts["w_rg_a"], wts["b_rg_a"], wts["w_rg_x"], wts["b_rg_x"],
             wts["rg_lambda"]]
    return pl.pallas_call(
        functools.partial(_rglru_kernel, nb=nb, tc=tc, has_state=conv_buf is not None),
        grid=(t // tc,),
        in_specs=in_specs,
        out_specs=[pl.BlockSpec((rows, w), lambda j: (j, 0)),
                   pl.BlockSpec((nb, w), lambda j: (0, 0)),
                   pl.BlockSpec((hist, w), lambda j: (0, 0))],
        out_shape=[jax.ShapeDtypeStruct((t * nb, w), out_dtype),
                   jax.ShapeDtypeStruct((nb, w), F32),
                   jax.ShapeDtypeStruct((hist, w), F32)],
        scratch_shapes=[pltpu.VMEM((rows + hist, w), F32), pltpu.VMEM((rows, w), F32), pltpu.VMEM((rows, w), F32)],
        compiler_params=_params(("arbitrary",)),
        name="rglru",
    )(*args)


def _merge_kernel(og, om, orr, mg, x_ref, mod_ref, g2, wbg, wbm, wbr, wo, y_ref, h2_ref):
    bb, tb, d = x_ref.shape
    gates = mg[...].astype(F32)
    merged = (_sigmoid(gates[:, 0:d]) * jnp.dot(og[...].astype(BF16), wbg[0], preferred_element_type=F32)
              + _sigmoid(gates[:, d:2 * d]) * jnp.dot(om[...].astype(BF16), wbm[0], preferred_element_type=F32)
              + _sigmoid(gates[:, 2 * d:3 * d]) * jnp.dot(orr[...].astype(BF16), wbr[0], preferred_element_type=F32))
    z = jnp.dot(merged.astype(BF16), wo[0], preferred_element_type=F32)
    mod = mod_ref[...]
    y = x_ref[...] + mod[:, :, 2 * d:3 * d] * z.reshape(bb, tb, d)
    y_ref[...] = y
    h2 = _rms(y, g2[...]) * (1.0 + mod[:, :, 4 * d:5 * d]) + mod[:, :, 3 * d:4 * d]
    h2_ref[...] = h2.reshape(bb * tb, d).astype(BF16)


def _merge(og, om, orr, mg, x, mod, g2, wts, layer, bb, tb, rg_time_major):
    bn, t, d = x.shape
    nt = t // tb
    rows = bb * tb
    rspec = lambda w: pl.BlockSpec((rows, w), lambda i, j: (i * nt + j, 0))
    if rg_time_major:
        assert bb == 1
        rg_spec = pl.BlockSpec((tb, d), lambda i, j: (j, i))
    else:
        rg_spec = rspec(d)
    wspec = _const_spec((1, d, d), lambda i, j: (layer, 0, 0))
    return pl.pallas_call(
        _merge_kernel,
        grid=(bn // bb, nt),
        in_specs=[rspec(d), rspec(d), rg_spec, rspec(3 * d),
                  pl.BlockSpec((bb, tb, d), lambda i, j: (i, j, 0)),
                  pl.BlockSpec((bb, 1, 6 * d), lambda i, j: (i, 0, 0)),
                  _const_spec((1, 1, d), lambda i, j: (layer, 0, 0)),
                  wspec, wspec, wspec, wspec],
        out_specs=[pl.BlockSpec((bb, tb, d), lambda i, j: (i, j, 0)), rspec(d)],
        out_shape=[jax.ShapeDtypeStruct((bn, t, d), F32), jax.ShapeDtypeStruct((bn * t, d), BF16)],
        compiler_params=_params(("arbitrary", "arbitrary")),
        name="merge",
    )(og, om, orr, mg, x, mod, g2, wts["w_br_gla"], wts["w_br_ml"], wts["w_br_rg"], wts["w_out"])


def _sort16_pairs():
    n, pairs, p = 16, [], 1
    while p < n:
        k = p
        while k >= 1:
            for j in range(k % p, n - k, 2 * k):
                for i in range(min(k, n - j - k)):
                    if (i + j) // (2 * p) == (i + j + k) // (2 * p):
                        pairs.append((i + j, i + j + k))
            k //= 2
        p *= 2
    return pairs


_SORT16 = _sort16_pairs()
_BITONIC16 = [(i, i + dd) for dd in (8, 4, 2, 1) for i in range(16) if (i & dd) == 0]
_CANDS = [(a, b) for a in range(PEER_TOPK) for b in range(PEER_TOPK) if (a + 1) * (b + 1) <= PEER_TOPK]


def _top16_sorted(st):
    xs = [st[g * 8:(g + 1) * 8, :] for g in range(16)]
    for i, j in _SORT16:
        hi, lo = jnp.maximum(xs[i], xs[j]), jnp.minimum(xs[i], xs[j])
        xs[i], xs[j] = hi, lo
    for shift in (4, 2, 1):
        ys = [jnp.maximum(xs[r], pltpu.roll(xs[15 - r], shift, 0)) for r in range(16)]
        for i, j in _BITONIC16:
            hi, lo = jnp.maximum(ys[i], ys[j]), jnp.minimum(ys[i], ys[j])
            ys[i], ys[j] = hi, lo
        xs = ys
    return xs


def _dup_bf16_words(x):
    bits = lax.bitcast_convert_type(x.astype(BF16).astype(F32), jnp.uint32)
    return bits | (bits >> 16)


def _peer_kernel(h2_ref, y_ref, mod_ref, wq_ref, keys_ref, u_ref, vt_ref, fg_ref, o_ref,
                 s0_ref, sv1_ref, r1b_ref, e1b_ref, n0w_ref, ccw_ref, pta_ref, ptb_ref, gt_ref, acc_ref,
                 *, final, sub):
    bb, tb, d = y_ref.shape
    tm = bb * tb
    ec = u_ref.shape[1]
    cidx = pl.program_id(1)
    nt_dims = (((1,), (1,)), ((), ()))
    ngrp = PEER_NKEYS // 8

    @pl.when(cidx == 0)
    def _():
        q = jnp.dot(h2_ref[...], wq_ref[0], preferred_element_type=F32)
        sub8 = lax.broadcasted_iota(jnp.int32, (8, tm), 0)
        tops = []
        for p in range(2):
            stack = [jnp.zeros((8, tm), F32) for _ in range(PEER_TOPK)]
            for h in range(PEER_HEADS):
                col = (h * 2 + p) * PEER_NKEYS
                qs = q[:, col:col + PEER_NKEYS].astype(BF16)
                st = lax.dot_general(keys_ref[0, h, p], qs, nt_dims, preferred_element_type=F32)
                srt = _top16_sorted(st)
                if p == 0:
                    s0_ref[h] = st
                else:
                    st3 = st.reshape(ngrp, 8, tm)
                    rank = jnp.zeros((ngrp, 8, tm), F32)
                    for r in range(PEER_TOPK):
                        sv1_ref[h, r] = srt[r]
                        rank = rank + jnp.where(srt[r][None] > st3, 1.0, 0.0)
                    r1b_ref[h] = rank.reshape(PEER_NKEYS, tm).astype(BF16)
                    e1b_ref[h] = jnp.exp(st3 - srt[0][None]).reshape(PEER_NKEYS, tm).astype(BF16)
                stack = [jnp.where(sub8 == h, srt[r], stack[r]) for r in range(PEER_TOPK)]
            tops.append(stack)
        cands = [tops[0][a] + tops[1][b] for a, b in _CANDS]
        work = list(cands)
        theta = jnp.full((8, tm), NEG, F32)
        cnt = jnp.zeros((8, tm), F32)
        for _ in range(PEER_TOPK):
            mx = functools.reduce(jnp.maximum, work)
            eq = [w == mx for w in work]
            theta = jnp.where(cnt < PEER_TOPK, mx, theta)
            cnt = cnt + functools.reduce(jnp.add, [e.astype(F32) for e in eq])
            work = [jnp.where(e, NEG, w) for e, w in zip(eq, work)]
        top = tops[0][0] + tops[1][0]
        z = functools.reduce(jnp.add, [jnp.where(cd >= theta, jnp.exp(cd - top), 0.0) for cd in cands])
        inv_z = 1.0 / z
        rep = lambda row: jnp.broadcast_to(row, (8, tm))[None]
        for h in range(PEER_HEADS):
            s03 = s0_ref[h].reshape(ngrp, 8, tm)
            th8 = rep(theta[h:h + 1, :])
            n0 = jnp.zeros((ngrp, 8, tm), F32)
            for b in range(PEER_TOPK):
                n0 = n0 + jnp.where(s03 + sv1_ref[h, b][None] >= th8, 1.0, 0.0)
            cc = jnp.exp(s03 - rep(tops[0][0][h:h + 1, :])) * rep(inv_z[h:h + 1, :])
            n0w = _dup_bf16_words(n0.reshape(PEER_NKEYS, tm))
            ccw = _dup_bf16_words(cc.reshape(PEER_NKEYS, tm))
            for l in range(tm // LANES):
                lanes = slice(l * LANES, (l + 1) * LANES)
                n0w_ref[h, l] = n0w[:, lanes]
                ccw_ref[h, l] = ccw[:, lanes]
        acc_ref[...] = jnp.zeros_like(acc_ref)
        ptb_ref[...] = jnp.zeros_like(ptb_ref)

    groups = ec // PEER_NKEYS
    nsub = PEER_NKEYS // sub
    prev = jnp.maximum(cidx - 1, 0)

    def step(pt_next, pt_cur):
        pt_next[...] = lax.dot_general(u_ref[0], h2_ref[...], nt_dims, preferred_element_type=F32)
        for g in range(groups):
            i = prev * groups + g
            for l in range(tm // LANES):
                lanes = slice(l * LANES, (l + 1) * LANES)
                rows = [(pltpu.bitcast(n0w_ref[h, l, pl.ds(i, sub // 2, stride=0), :], BF16),
                         pltpu.bitcast(ccw_ref[h, l, pl.ds(i, sub // 2, stride=0), :], BF16))
                        for h in range(PEER_HEADS)]
                for jb in range(nsub):
                    r0 = g * PEER_NKEYS + jb * sub
                    j0 = jb * sub
                    pre = pt_cur[r0:r0 + sub, lanes]
                    act = (0.5 * pre * (1.0 + lax.erf(pre * 0.7071067811865476))).astype(BF16)
                    wgt = jnp.zeros((sub, LANES), BF16)
                    for h in range(PEER_HEADS):
                        val = e1b_ref[h, j0:j0 + sub, lanes] * rows[h][1]
                        wgt = wgt + jnp.where(r1b_ref[h, j0:j0 + sub, lanes] < rows[h][0], val, jnp.zeros_like(val))
                    gt_ref[r0:r0 + sub, lanes] = act * wgt
        acc_ref[...] += jnp.dot(vt_ref[0], gt_ref[...], preferred_element_type=F32)

    @pl.when(cidx % 2 == 0)
    def _():
        step(pta_ref, ptb_ref)

    @pl.when(cidx % 2 == 1)
    def _():
        step(ptb_ref, pta_ref)

    @pl.when(cidx == pl.num_programs(1) - 1)
    def _():
        peer = acc_ref[...].T.reshape(bb, tb, d)
        y = y_ref[...] + mod_ref[...][:, :, 5 * d:6 * d] * peer
        if final:
            y = _rms(y, fg_ref[...])
        o_ref[...] = y


def _peer(h2, y, mod, wts, final_g, layer, bb, tb, ec, final):
    bn, t, d = y.shape
    nt = t // tb
    tm = bb * tb
    n_exp = wts["peer_u"].shape[1]
    nq = wts["w_peer_q"].shape[-1]
    sub = 32
    assert tm % LANES == 0
    tile = lambda dt: pltpu.VMEM((PEER_HEADS, PEER_NKEYS, tm), dt)
    row_tile = lambda: pltpu.VMEM((PEER_HEADS, tm // LANES, PEER_NKEYS, LANES), jnp.uint32)
    nc = n_exp // ec
    return pl.pallas_call(
        functools.partial(_peer_kernel, final=final, sub=sub),
        grid=(bn * t // tm, nc + 1),
        in_specs=[pl.BlockSpec((tm, d), lambda i, c: (i, 0)),
                  pl.BlockSpec((bb, tb, d), lambda i, c: (i // nt, i % nt, 0)),
                  pl.BlockSpec((bb, 1, 6 * d), lambda i, c: (i // nt, 0, 0)),
                  _const_spec((1, d, nq), lambda i, c: (layer, 0, 0)),
                  _const_spec((1, PEER_HEADS, 2, PEER_NKEYS, PEER_NKEYS), lambda i, c: (layer, 0, 0, 0, 0)),
                  pl.BlockSpec((1, ec, d), lambda i, c: (layer, jnp.minimum(c, nc - 1), 0)),
                  pl.BlockSpec((1, d, ec), lambda i, c: (layer, 0, jnp.maximum(c - 1, 0))),
                  _const_spec((1, 1, d), lambda i, c: (0, 0, 0))],
        out_specs=pl.BlockSpec((bb, tb, d), lambda i, c: (i // nt, i % nt, 0)),
        out_shape=jax.ShapeDtypeStruct((bn, t, d), F32),
        scratch_shapes=[tile(F32),
                        pltpu.VMEM((PEER_HEADS, PEER_TOPK, 8, tm), F32),
                        tile(BF16), tile(BF16), row_tile(), row_tile(),
                        pltpu.VMEM((ec, tm), F32),
                        pltpu.VMEM((ec, tm), F32),
                        pltpu.VMEM((ec, tm), BF16),
                        pltpu.VMEM((d, tm), F32)],
        compiler_params=_params(("arbitrary", "arbitrary")),
        name="peer",
    )(h2, y, mod, wts["w_peer_q"], wts["peer_keys"], wts["peer_u"], wts["peer_vt"], final_g)


def _prepare_weights(w_in, w_gla_f2, b_gla_f, gla_norm_g, b_ml_i, b_ml_f, ml_norm_g, conv_w, conv_b, w_rg_a, b_rg_a,
                     w_rg_x, b_rg_x, rg_lambda, w_br_gla, w_br_ml, w_br_rg, w_out, w_peer_q, peer_keys, peer_u, peer_v,
                     norm1_g, norm2_g):
    n_layers, d, _ = w_in.shape
    offs, o = {}, 0
    for name, w in (("gq", 512), ("gk", 512), ("gv", 1024), ("gg", 1024), ("gf", GLA_RANK),
                    ("mq", 512), ("mk", 512), ("mv", 1024), ("mo", 1024), ("mi", N_HEADS), ("mf", N_HEADS),
                    ("rx", 1024), ("mg", 3072)):
        offs[name] = (o, o + w)
        o += w
    assert o == w_in.shape[-1]
    cut = lambda name: w_in[:, :, offs[name][0]:offs[name][1]]
    w_main = jnp.concatenate([cut(n) for n, _ in IN_PIECES], axis=-1).astype(BF16)
    n_small = GLA_RANK + 2 * N_HEADS
    w_small = jnp.concatenate([cut("gf"), cut("mi"), cut("mf"),
                               jnp.zeros((n_layers, d, SMALL_W - n_small), F32)], axis=-1).astype(BF16)
    wf2 = jnp.concatenate([w_gla_f2, jnp.zeros((n_layers, SMALL_W - GLA_RANK, w_gla_f2.shape[-1]), F32)],
                          axis=1).astype(BF16)
    bias_small = jnp.concatenate([jnp.zeros((n_layers, GLA_RANK), F32), b_ml_i, b_ml_f,
                                  jnp.zeros((n_layers, SMALL_W - n_small), F32)], axis=-1)
    row = lambda a: a.reshape(n_layers, 1, a.shape[-1])
    return dict(
        w_main=w_main, w_small=w_small, wf2=wf2, b_gla_f=row(b_gla_f), gla_norm_g=row(gla_norm_g),
        bias_small=row(bias_small), ml_norm_g=row(ml_norm_g),
        conv_w=conv_w, conv_b=row(conv_b), w_rg_a=w_rg_a.astype(BF16), b_rg_a=row(b_rg_a),
        w_rg_x=w_rg_x.astype(BF16), b_rg_x=row(b_rg_x), rg_lambda=row(rg_lambda),
        w_br_gla=w_br_gla.astype(BF16), w_br_ml=w_br_ml.astype(BF16), w_br_rg=w_br_rg.astype(BF16),
        w_out=w_out.astype(BF16), w_peer_q=w_peer_q.astype(BF16), peer_keys=peer_keys.astype(BF16),
        peer_u=peer_u.astype(BF16), peer_vt=jnp.swapaxes(peer_v, 1, 2).astype(BF16),
        norm1_g=row(norm1_g), norm2_g=row(norm2_g))


def _layer(x, mod, state, wts, final_g, layer, final, cfg):
    bn, t, d = x.shape
    fresh = state is None
    act_dtype = BF16 if fresh else F32
    outs = _in_projection(x, mod, wts["norm1_g"], wts["w_main"], wts["w_small"], layer,
                          cfg["bb"], cfg["tb"], act_dtype, rx_time_major=fresh)
    p = {name: o for (name, _), o in zip(IN_PIECES, outs[:-1])}
    small = outs[-1]
    if fresh:
        s_gla = ml_state = None
        rx = p["rx"].reshape(t * bn, d)
        conv_buf = h0 = None
    else:
        s_gla, c0, n0, m0, h0, conv_buf = state
        ml_state = (c0, n0, m0)
        rx = jnp.swapaxes(p["rx"].reshape(bn, t, d), 0, 1).reshape(t * bn, d)
    o_gla, s_gla_new = _gla_mixer(p, small, wts["wf2"], wts["b_gla_f"], wts["gla_norm_g"], s_gla, layer,
                                  bn, t, cfg["gla_bb"], act_dtype)
    o_ml, c_new, n_new, m_new = _mlstm_mixer(p, small, wts["bias_small"], wts["ml_norm_g"], ml_state, layer,
                                             bn, t, cfg["ml_bb"], act_dtype)
    hs, h_fin, buf_new = _rglru(rx, conv_buf, h0, wts, layer, bn, t, cfg["rg_tc"], act_dtype)
    if fresh:
        o_rg = hs.reshape(t, bn * d)
    else:
        o_rg = jnp.swapaxes(hs.reshape(t, bn, d), 0, 1).reshape(bn * t, d)
    y, h2 = _merge(o_gla, o_ml, o_rg, p["mg"], x, mod, wts["norm2_g"], wts, layer, cfg["bb"], cfg["tb"],
                   rg_time_major=fresh)
    out = _peer(h2, y, mod, wts, final_g, layer, cfg["peer_bb"], cfg["peer_tb"], cfg["peer_ec"], final)
    new_state = (s_gla_new, c_new, n_new, m_new[:, 0, :N_HEADS], h_fin,
                 jnp.swapaxes(buf_new.reshape(CONV_W - 1, bn, d), 0, 1))
    return out, new_state


def kernel(x_prompt, x_sample, state_gla, state_mlstm_c, state_mlstm_n, state_mlstm_m, state_rglru_h, state_rglru_conv, c_prompt, c_sample, w_ada, b_ada, norm1_g, norm2_g, w_in, w_gla_f2, b_gla_f, gla_norm_g, b_ml_i, b_ml_f, ml_norm_g, conv_w, conv_b, w_rg_a, b_rg_a, w_rg_x, b_rg_x, rg_lambda, w_br_gla, w_br_ml, w_br_rg, w_out, w_peer_q, peer_keys, peer_u, peer_v, final_norm_g):
    n_layers = w_in.shape[0]
    bp, tp, d = x_prompt.shape
    bs, ts, _ = x_sample.shape
    wts = _prepare_weights(w_in, w_gla_f2, b_gla_f, gla_norm_g, b_ml_i, b_ml_f, ml_norm_g, conv_w, conv_b, w_rg_a,
                           b_rg_a, w_rg_x, b_rg_x, rg_lambda, w_br_gla, w_br_ml, w_br_rg, w_out, w_peer_q,
                           peer_keys, peer_u, peer_v, norm1_g, norm2_g)
    final_g = final_norm_g.reshape(1, 1, d)
    mod_all = _modulation(jnp.concatenate([c_prompt, c_sample], axis=0), w_ada, b_ada)
    m0_pad = jnp.pad(state_mlstm_m, ((0, 0), (0, 0), (0, SMALL_W - N_HEADS))).reshape(n_layers, bs, 1, SMALL_W)
    conv_tm = jnp.swapaxes(state_rglru_conv, 1, 2).reshape(n_layers, (CONV_W - 1) * bs, d)
    sample_state = (state_gla, state_mlstm_c, state_mlstm_n, m0_pad, state_rglru_h, conv_tm)

    cfg_p = dict(bb=1, tb=min(256, tp), gla_bb=min(4, bp), ml_bb=min(2, bp), rg_tc=min(128, tp), peer_bb=1,
                 peer_tb=min(512, tp), peer_ec=1024)
    cfg_s = dict(bb=min(16, bs), tb=ts, gla_bb=min(4, bs), ml_bb=min(4, bs), rg_tc=ts, peer_bb=min(512 // ts, bs),
                 peer_tb=ts, peer_ec=1024)
    yp, ys = x_prompt, x_sample
    new_p, new_s = [], []
    for l in range(n_layers):
        final = l == n_layers - 1
        mod_p = mod_all[l, :bp].reshape(bp, 1, 6 * d)
        mod_s = mod_all[l, bp:].reshape(bs, 1, 6 * d)
        yp, st_p = _layer(yp, mod_p, None, wts, final_g, l, final, cfg_p)
        ys, st_s = _layer(ys, mod_s, sample_state, wts, final_g, l, final, cfg_s)
        new_p.append(st_p)
        new_s.append(st_s)
    stacked_p = [jnp.stack(z) for z in zip(*new_p)]
    stacked_s = [jnp.stack(z) for z in zip(*new_s)]
    return (yp, ys, *stacked_p, *stacked_s)
```

```python
import functools

import jax
import jax.numpy as jnp
from jax import lax
from jax.experimental import pallas as pl
from jax.experimental.pallas import tpu as pltpu

F32 = jnp.float32
BF16 = jnp.bfloat16
HIGHEST = lax.Precision.HIGHEST

EPS = 1e-6
N_HEADS = 4
DK = 128
DV = 256
GLA_RANK = 16
GLA_GATE_TEMP = 16.0
CHUNK = 64
RG_BLOCKS = 8
RG_C = 8.0
CONV_W = 4
PEER_HEADS = 8
PEER_NKEYS = 128
PEER_TOPK = 16
NEG = -1e30
LANES = 128

VMEM_LIMIT_BYTES = 56 * 1024 * 1024

IN_PIECES = (("gq", 512), ("gk", 512), ("gv", 1024), ("gg", 1024),
             ("mq", 512), ("mk", 512), ("mv", 1024), ("mo", 1024),
             ("rx", 1024), ("mg", 3072))
SMALL_W = 128
SM_I = GLA_RANK
SM_F = GLA_RANK + N_HEADS


def _params(sem):
    return pltpu.CompilerParams(dimension_semantics=sem, vmem_limit_bytes=VMEM_LIMIT_BYTES)


def _const_spec(shape, index_map):
    return pl.BlockSpec(shape, index_map, pipeline_mode=pl.Buffered(1))


def _log_sigmoid(x):
    return jnp.minimum(x, 0.0) - jnp.log1p(jnp.exp(-jnp.abs(x)))


def _softplus(x):
    return jnp.maximum(x, 0.0) + jnp.log1p(jnp.exp(-jnp.abs(x)))


def _sigmoid(x):
    return 1.0 / (1.0 + jnp.exp(-x))


def _rms(x, g):
    return x * lax.rsqrt(jnp.mean(x * x, axis=-1, keepdims=True) + EPS) * g


def _pad_rows(a, rows, value=0.0):
    if a.shape[0] == rows:
        return a
    return jnp.concatenate([a, jnp.full((rows - a.shape[0], a.shape[1]), value, a.dtype)], axis=0)


def _row_to_col(row, eye):
    n = row.shape[1]
    return jnp.sum(jnp.where(eye, jnp.broadcast_to(row, (n, n)), 0.0), axis=1, keepdims=True)


def _col_to_row(col, eye):
    n = col.shape[0]
    return jnp.sum(jnp.where(eye, jnp.broadcast_to(col, (n, n)), 0.0), axis=0, keepdims=True)


def _mod_kernel(c_ref, w_ref, b_ref, o_ref):
    c = c_ref[...]
    s = (c * _sigmoid(c)).astype(BF16)
    o_ref[0] = jnp.dot(s, w_ref[0].astype(BF16), preferred_element_type=F32) + b_ref[0]


def _modulation(c_all, w_ada, b_ada):
    n_layers, d, n6 = w_ada.shape
    rows = c_all.shape[0]
    tn = 1536
    return pl.pallas_call(
        _mod_kernel,
        grid=(n_layers, n6 // tn),
        in_specs=[pl.BlockSpec((rows, d), lambda l, n: (0, 0)),
                  pl.BlockSpec((1, d, tn), lambda l, n: (l, 0, n)),
                  pl.BlockSpec((1, 1, tn), lambda l, n: (l, 0, n))],
        out_specs=pl.BlockSpec((1, rows, tn), lambda l, n: (l, 0, n)),
        out_shape=jax.ShapeDtypeStruct((n_layers, rows, n6), F32),
        compiler_params=_params(("arbitrary", "arbitrary")),
        name="adaln_mod",
    )(c_all, w_ada, b_ada.reshape(n_layers, 1, n6))


def _inproj_kernel(x_ref, mod_ref, g_ref, wm_ref, ws_ref, *out_refs):
    bb, tb, d = x_ref.shape
    mod = mod_ref[...]
    h = _rms(x_ref[...], g_ref[...]) * (1.0 + mod[:, :, d:2 * d]) + mod[:, :, 0:d]
    hb = h.reshape(bb * tb, d).astype(BF16)
    off = 0
    for r, (_, w) in zip(out_refs[:-1], IN_PIECES):
        r[...] = jnp.dot(hb, wm_ref[0, :, off:off + w], preferred_element_type=F32).astype(r.dtype)
        off += w
    out_refs[-1][...] = jnp.dot(hb, ws_ref[0], preferred_element_type=F32)


def _in_projection(x, mod, g1, w_main, w_small, layer, bb, tb, out_dtype, rx_time_major):
    bn, t, d = x.shape
    nt = t // tb
    rows = bb * tb
    n_main = w_main.shape[-1]
    out_shapes, out_specs = [], []
    for name, w in IN_PIECES:
        if name == "rx" and rx_time_major:
            out_shapes.append(jax.ShapeDtypeStruct((t, bn * w), out_dtype))
            out_specs.append(pl.BlockSpec((tb, w), lambda i, j: (j, i)))
        else:
            out_shapes.append(jax.ShapeDtypeStruct((bn * t, w), out_dtype))
            out_specs.append(pl.BlockSpec((rows, w), lambda i, j: (i * nt + j, 0)))
    out_shapes.append(jax.ShapeDtypeStruct((bn * t, SMALL_W), F32))
    out_specs.append(pl.BlockSpec((rows, SMALL_W), lambda i, j: (i * nt + j, 0)))
    if rx_time_major:
        assert bb == 1
    return pl.pallas_call(
        _inproj_kernel,
        grid=(bn // bb, nt),
        in_specs=[pl.BlockSpec((bb, tb, d), lambda i, j: (i, j, 0)),
                  pl.BlockSpec((bb, 1, 6 * d), lambda i, j: (i, 0, 0)),
                  _const_spec((1, 1, d), lambda i, j: (layer, 0, 0)),
                  _const_spec((1, d, n_main), lambda i, j: (layer, 0, 0)),
                  _const_spec((1, d, SMALL_W), lambda i, j: (layer, 0, 0))],
        out_specs=out_specs,
        out_shape=out_shapes,
        compiler_params=_params(("arbitrary", "arbitrary")),
        name="in_proj",
    )(x, mod, g1, w_main, w_small)


def _gla_kernel(*refs, bb, tc, has_state):
    if has_state:
        gq, gk, gv, gg, sm, wf2, bf, gn, s0, o_ref, s_ref = refs
    else:
        gq, gk, gv, gg, sm, wf2, bf, gn, o_ref, s_ref = refs
        s0 = None
    c = CHUNK

    @pl.when(pl.program_id(1) == 0)
    def _():
        if has_state:
            s_ref[...] = s0[...]
        else:
            s_ref[...] = jnp.zeros_like(s_ref)

    ri = lax.broadcasted_iota(jnp.int32, (c, c), 0)
    ci = lax.broadcasted_iota(jnp.int32, (c, c), 1)
    causal = ri >= ci
    tril = causal.astype(F32)
    hk = N_HEADS * DK
    valid = lax.broadcasted_iota(jnp.int32, (c, hk), 0) < tc
    eye = (lax.broadcasted_iota(jnp.int32, (DK, DK), 0) == lax.broadcasted_iota(jnp.int32, (DK, DK), 1))
    nt_dims = (((1,), (1,)), ((), ()))
    tn_dims = (((0,), (0,)), ((), ()))
    for s in range(bb):
        smb = _pad_rows(sm[s], c).astype(BF16)
        x = jnp.dot(smb, wf2[0], preferred_element_type=F32) + bf[0]
        logf = jnp.where(valid, _log_sigmoid(x) * (1.0 / GLA_GATE_TEMP), 0.0)
        b = jnp.dot(tril, logf, precision=HIGHEST, preferred_element_type=F32)
        b_last = b[c - 1:c, :]
        q = _pad_rows(gq[s].astype(F32), c) * (DK ** -0.5)
        k = _pad_rows(gk[s].astype(F32), c)
        qd = (q * jnp.exp(b)).astype(BF16)
        kd = (k * jnp.exp(-b)).astype(BF16)
        kend = (k * jnp.exp(b_last - b)).astype(BF16)
        vb = _pad_rows(gv[s].astype(F32), c).astype(BF16)
        g = gg[s].astype(F32)
        g = g * _sigmoid(g)
        for h in range(N_HEADS):
            kcols = slice(h * DK, (h + 1) * DK)
            vcols = slice(h * DV, (h + 1) * DV)
            att = lax.dot_general(qd[:, kcols], kd[:, kcols], nt_dims, preferred_element_type=F32)
            att = jnp.where(causal, att, 0.0).astype(BF16)
            st = s_ref[s, h]
            o = (jnp.dot(att, vb[:, vcols], preferred_element_type=F32)
                 + jnp.dot(qd[:, kcols], st.astype(BF16), preferred_element_type=F32))
            dec = jnp.exp(_row_to_col(b_last[:, kcols], eye))
            s_ref[s, h] = dec * st + lax.dot_general(kend[:, kcols], vb[:, vcols], tn_dims,
                                                     preferred_element_type=F32)
            o_ref[s, :, vcols] = (_rms(o[:tc], gn[0]) * g[:, vcols]).astype(o_ref.dtype)


def _gla_mixer(p, small, wf2, bfv, gn, s0, layer, bn, t, bb, out_dtype):
    tc = min(CHUNK, t)
    nch = t // tc
    assert tc * nch == t
    rspec = lambda w: pl.BlockSpec((bb, tc, w), lambda i, j: (i, j, 0))
    r3 = lambda a: a.reshape(bn, t, a.shape[-1])
    in_specs = [rspec(512), rspec(512), rspec(1024), rspec(1024), rspec(SMALL_W),
                _const_spec((1, SMALL_W, 512), lambda i, j: (layer, 0, 0)),
                _const_spec((1, 1, 512), lambda i, j: (layer, 0, 0)),
                _const_spec((1, 1, DV), lambda i, j: (layer, 0, 0))]
    args = [r3(p["gq"]), r3(p["gk"]), r3(p["gv"]), r3(p["gg"]), r3(small), wf2, bfv, gn]
    sspec = pl.BlockSpec((bb, N_HEADS, DK, DV), lambda i, j: (i, 0, 0, 0))
    if s0 is not None:
        in_specs.append(pl.BlockSpec((None, bb, N_HEADS, DK, DV), lambda i, j: (layer, i, 0, 0, 0)))
        args.append(s0)
    o, s_new = pl.pallas_call(
        functools.partial(_gla_kernel, bb=bb, tc=tc, has_state=s0 is not None),
        grid=(bn // bb, nch),
        in_specs=in_specs,
        out_specs=[rspec(1024), sspec],
        out_shape=[jax.ShapeDtypeStruct((bn, t, 1024), out_dtype),
                   jax.ShapeDtypeStruct((bn, N_HEADS, DK, DV), F32)],
        compiler_params=_params(("arbitrary", "arbitrary")),
        name="gla_mixer",
    )(*args)
    return o.reshape(bn * t, 1024), s_new


def _mlstm_kernel(*refs, bb, tc, has_state):
    if has_state:
        mq, mk, mv, mo, sm, bias, gn, c0, n0, m0, o_ref, c_ref, n_ref, m_ref = refs
    else:
        mq, mk, mv, mo, sm, bias, gn, o_ref, c_ref, n_ref, m_ref = refs
    c = CHUNK

    @pl.when(pl.program_id(1) == 0)
    def _():
        if has_state:
            c_ref[...] = c0[...]
            n_ref[...] = n0[...]
            m_ref[...] = m0[...]
        else:
            c_ref[...] = jnp.zeros_like(c_ref)
            n_ref[...] = jnp.zeros_like(n_ref)
            m_ref[...] = jnp.zeros_like(m_ref)

    ri = lax.broadcasted_iota(jnp.int32, (c, c), 0)
    ci = lax.broadcasted_iota(jnp.int32, (c, c), 1)
    causal = ri >= ci
    eye = ri == ci
    tril = causal.astype(F32)
    valid = lax.broadcasted_iota(jnp.int32, (c, SMALL_W), 0) < tc
    lane = lax.broadcasted_iota(jnp.int32, (1, SMALL_W), 1)
    nt_dims = (((1,), (1,)), ((), ()))
    tn_dims = (((0,), (0,)), ((), ()))
    for s in range(bb):
        t1 = _pad_rows(sm[s], c) + bias[0]
        ig_t = jnp.where(valid, t1, NEG)
        lf_t = jnp.where(valid, _log_sigmoid(t1), 0.0)
        f_t = jnp.dot(tril, lf_t, precision=HIGHEST, preferred_element_type=F32)
        m_prev = m_ref[s]
        m_new_row = m_prev
        qb_all = _pad_rows(mq[s].astype(F32), c).astype(BF16)
        k_all = _pad_rows(mk[s].astype(F32), c) * (DK ** -0.5)
        vb_all = _pad_rows(mv[s].astype(F32), c).astype(BF16)
        g_all = _sigmoid(mo[s].astype(F32))
        for h in range(N_HEADS):
            kcols = slice(h * DK, (h + 1) * DK)
            vcols = slice(h * DV, (h + 1) * DV)
            qb = qb_all[:, kcols]
            q = qb.astype(F32)
            k = k_all[:, kcols]
            vb = vb_all[:, vcols]
            ic = ig_t[:, SM_I + h:SM_I + h + 1]
            fc = f_t[:, SM_F + h:SM_F + h + 1]
            ms = m_prev[:, h:h + 1]
            a = ic - fc
            a_row = _col_to_row(a, eye)
            cm = jnp.max(jnp.where(causal, jnp.broadcast_to(a_row, (c, c)), NEG), axis=1, keepdims=True)
            m = fc + jnp.maximum(ms, cm)
            w_inter = jnp.exp(fc + ms - m)
            dmat = jnp.exp(jnp.where(causal, (fc - m) + a_row, NEG))
            qk = lax.dot_general(qb, k.astype(BF16), nt_dims, preferred_element_type=F32) * dmat
            cs = c_ref[s, h]
            ns = n_ref[s, h:h + 1, :]
            num = (jnp.dot(qk.astype(BF16), vb, preferred_element_type=F32)
                   + w_inter * jnp.dot(qb, cs.astype(BF16), preferred_element_type=F32))
            den = jnp.sum(qk, axis=1, keepdims=True) + w_inter * jnp.sum(q * ns, axis=1, keepdims=True)
            hh = num / jnp.maximum(jnp.abs(den), jnp.exp(-m))
            m_last = m[c - 1:c, :]
            f_last = fc[c - 1:c, :]
            w_state = jnp.exp(ic + f_last - fc - m_last)
            carry = jnp.exp(f_last + ms - m_last)
            kw = k * w_state
            c_ref[s, h] = carry * cs + lax.dot_general(kw.astype(BF16), vb, tn_dims, preferred_element_type=F32)
            n_ref[s, h:h + 1, :] = carry * ns + jnp.sum(kw, axis=0, keepdims=True)
            m_new_row = jnp.where(lane == h, m_last, m_new_row)
            o_ref[s, :, vcols] = (_rms(hh[:tc], gn[0]) * g_all[:, vcols]).astype(o_ref.dtype)
        m_ref[s] = m_new_row


def _mlstm_mixer(p, small, bias, gn, state, layer, bn, t, bb, out_dtype):
    tc = min(CHUNK, t)
    nch = t // tc
    assert tc * nch == t
    rspec = lambda w: pl.BlockSpec((bb, tc, w), lambda i, j: (i, j, 0))
    r3 = lambda a: a.reshape(bn, t, a.shape[-1])
    in_specs = [rspec(512), rspec(512), rspec(1024), rspec(1024), rspec(SMALL_W),
                _const_spec((1, 1, SMALL_W), lambda i, j: (layer, 0, 0)),
                _const_spec((1, 1, DV), lambda i, j: (layer, 0, 0))]
    args = [r3(p["mq"]), r3(p["mk"]), r3(p["mv"]), r3(p["mo"]), r3(small), bias, gn]
    cspec = pl.BlockSpec((bb, N_HEADS, DK, DV), lambda i, j: (i, 0, 0, 0))
    nspec = pl.BlockSpec((bb, N_HEADS, DK), lambda i, j: (i, 0, 0))
    mspec = pl.BlockSpec((bb, 1, SMALL_W), lambda i, j: (i, 0, 0))
    if state is not None:
        c0, n0, m0 = state
        in_specs += [pl.BlockSpec((None, bb, N_HEADS, DK, DV), lambda i, j: (layer, i, 0, 0, 0)),
                     pl.BlockSpec((None, bb, N_HEADS, DK), lambda i, j: (layer, i, 0, 0)),
                     pl.BlockSpec((None, bb, 1, SMALL_W), lambda i, j: (layer, i, 0, 0))]
        args += [c0, n0, m0]
    o, c_new, n_new, m_new = pl.pallas_call(
        functools.partial(_mlstm_kernel, bb=bb, tc=tc, has_state=state is not None),
        grid=(bn // bb, nch),
        in_specs=in_specs,
        out_specs=[rspec(1024), cspec, nspec, mspec],
        out_shape=[jax.ShapeDtypeStruct((bn, t, 1024), out_dtype),
                   jax.ShapeDtypeStruct((bn, N_HEADS, DK, DV), F32),
                   jax.ShapeDtypeStruct((bn, N_HEADS, DK), F32),
                   jax.ShapeDtypeStruct((bn, 1, SMALL_W), F32)],
        compiler_params=_params(("arbitrary", "arbitrary")),
        name="mlstm_mixer",
    )(*args)
    return o.reshape(bn * t, 1024), c_new, n_new, m_new


def _rglru_kernel(*refs, nb, tc, has_state):
    if has_state:
        rx, cb, h0, cw, cbias, wa, ba, wx, bx, lam, hs_ref, hf_ref, nb_ref, xpad, a_s, u_s = refs
    else:
        rx, cw, cbias, wa, ba, wx, bx, lam, hs_ref, hf_ref, nb_ref, xpad, a_s, u_s = refs
    hist = (CONV_W - 1) * nb
    rows = tc * nb
    w = rx.shape[1]
    bw = w // RG_BLOCKS

    @pl.when(pl.program_id(0) == 0)
    def _():
        if has_state:
            xpad[0:hist, :] = cb[0]
            hf_ref[...] = h0[0]
        else:
            xpad[0:hist, :] = jnp.zeros((hist, w), F32)
            hf_ref[...] = jnp.zeros_like(hf_ref)

    xpad[hist:hist + rows, :] = rx[...].astype(F32)
    xc = cbias[0]
    for j in range(CONV_W):
        xc = xc + xpad[j * nb:j * nb + rows, :] * cw[0, j:j + 1, :]
    sp = _softplus(-lam[0])
    for n in range(RG_BLOCKS):
        cols = slice(n * bw, (n + 1) * bw)
        xn = xc[:, cols]
        xg = xn.astype(BF16)
        r = _sigmoid(jnp.dot(xg, wa[0, n], preferred_element_type=F32) + ba[0, :, cols])
        i = _sigmoid(jnp.dot(xg, wx[0, n], preferred_element_type=F32) + bx[0, :, cols])
        log_a = -RG_C * r * sp[:, cols]
        a_s[:, cols] = jnp.exp(log_a)
        u_s[:, cols] = jnp.sqrt(1.0 - jnp.exp(2.0 * log_a)) * (i * xn)

    def body(t, h):
        o = pl.multiple_of(t * nb, nb)
        h = a_s[pl.ds(o, nb), :] * h + u_s[pl.ds(o, nb), :]
        hs_ref[pl.ds(o, nb), :] = h.astype(hs_ref.dtype)
        return h

    hf_ref[...] = lax.fori_loop(0, tc, body, hf_ref[...])
    tail = xpad[rows:rows + hist, :]
    xpad[0:hist, :] = tail
    nb_ref[...] = tail


def _rglru(rx, conv_buf, h0, wts, layer, nb, t, tc, out_dtype):
    w = rx.shape[1]
    rows = tc * nb
    hist = (CONV_W - 1) * nb
    bw = w // RG_BLOCKS
    lsp = lambda shape: _const_spec((1,) + shape, lambda j: (layer,) + (0,) * len(shape))
    in_specs = [pl.BlockSpec((rows, w), lambda j: (j, 0))]
    args = [rx]
    if conv_buf is not None:
        in_specs += [lsp((hist, w)), lsp((nb, w))]
        args += [conv_buf, h0]
    in_specs += [lsp((CONV_W, w)), lsp((1, w)), lsp((RG_BLOCKS, bw, bw)), lsp((1, w)),
                 lsp((RG_BLOCKS, bw, bw)), lsp((1, w)), lsp((1, w))]
    args += [wts["conv_w"], wts["conv_b"], wts["w_rg_a"], wts["b_rg_a"], wts["w_rg_x"], wts["b_rg_x"],
             wts["rg_lambda"]]
    return pl.pallas_call(
        functools.partial(_rglru_kernel, nb=nb, tc=tc, has_state=conv_buf is not None),
        grid=(t // tc,),
        in_specs=in_specs,
        out_specs=[pl.BlockSpec((rows, w), lambda j: (j, 0)),
                   pl.BlockSpec((nb, w), lambda j: (0, 0)),
                   pl.BlockSpec((hist, w), lambda j: (0, 0))],
        out_shape=[jax.ShapeDtypeStruct((t * nb, w), out_dtype),
                   jax.ShapeDtypeStruct((nb, w), F32),
                   jax.ShapeDtypeStruct((hist, w), F32)],
        scratch_shapes=[pltpu.VMEM((rows + hist, w), F32), pltpu.VMEM((rows, w), F32), pltpu.VMEM((rows, w), F32)],
        compiler_params=_params(("arbitrary",)),
        name="rglru",
    )(*args)


def _merge_kernel(og, om, orr, mg, x_ref, mod_ref, g2, wbg, wbm, wbr, wo, y_ref, h2_ref):
    bb, tb, d = x_ref.shape
    gates = mg[...].astype(F32)
    merged = (_sigmoid(gates[:, 0:d]) * jnp.dot(og[...].astype(BF16), wbg[0], preferred_element_type=F32)
              + _sigmoid(gates[:, d:2 * d]) * jnp.dot(om[...].astype(BF16), wbm[0], preferred_element_type=F32)
              + _sigmoid(gates[:, 2 * d:3 * d]) * jnp.dot(orr[...].astype(BF16), wbr[0], preferred_element_type=F32))
    z = jnp.dot(merged.astype(BF16), wo[0], preferred_element_type=F32)
    mod = mod_ref[...]
    y = x_ref[...] + mod[:, :, 2 * d:3 * d] * z.reshape(bb, tb, d)
    y_ref[...] = y
    h2 = _rms(y, g2[...]) * (1.0 + mod[:, :, 4 * d:5 * d]) + mod[:, :, 3 * d:4 * d]
    h2_ref[...] = h2.reshape(bb * tb, d).astype(BF16)


def _merge(og, om, orr, mg, x, mod, g2, wts, layer, bb, tb, rg_time_major):
    bn, t, d = x.shape
    nt = t // tb
    rows = bb * tb
    rspec = lambda w: pl.BlockSpec((rows, w), lambda i, j: (i * nt + j, 0))
    if rg_time_major:
        assert bb == 1
        rg_spec = pl.BlockSpec((tb, d), lambda i, j: (j, i))
    else:
        rg_spec = rspec(d)
    wspec = _const_spec((1, d, d), lambda i, j: (layer, 0, 0))
    return pl.pallas_call(
        _merge_kernel,
        grid=(bn // bb, nt),
        in_specs=[rspec(d), rspec(d), rg_spec, rspec(3 * d),
                  pl.BlockSpec((bb, tb, d), lambda i, j: (i, j, 0)),
                  pl.BlockSpec((bb, 1, 6 * d), lambda i, j: (i, 0, 0)),
                  _const_spec((1, 1, d), lambda i, j: (layer, 0, 0)),
                  wspec, wspec, wspec, wspec],
        out_specs=[pl.BlockSpec((bb, tb, d), lambda i, j: (i, j, 0)), rspec(d)],
        out_shape=[jax.ShapeDtypeStruct((bn, t, d), F32), jax.ShapeDtypeStruct((bn * t, d), BF16)],
        compiler_params=_params(("arbitrary", "arbitrary")),
        name="merge",
    )(og, om, orr, mg, x, mod, g2, wts["w_br_gla"], wts["w_br_ml"], wts["w_br_rg"], wts["w_out"])


def _sort16_pairs():
    n, pairs, p = 16, [], 1
    while p < n:
        k = p
        while k >= 1:
            for j in range(k % p, n - k, 2 * k):
                for i in range(min(k, n - j - k)):
                    if (i + j) // (2 * p) == (i + j + k) // (2 * p):
                        pairs.append((i + j, i + j + k))
            k //= 2
        p *= 2
    return pairs


_SORT16 = _sort16_pairs()
_BITONIC16 = [(i, i + dd) for dd in (8, 4, 2, 1) for i in range(16) if (i & dd) == 0]
_CANDS = [(a, b) for a in range(PEER_TOPK) for b in range(PEER_TOPK) if (a + 1) * (b + 1) <= PEER_TOPK]


def _top16_sorted(st):
    xs = [st[g * 8:(g + 1) * 8, :] for g in range(16)]
    for i, j in _SORT16:
        hi, lo = jnp.maximum(xs[i], xs[j]), jnp.minimum(xs[i], xs[j])
        xs[i], xs[j] = hi, lo
    for shift in (4, 2, 1):
        ys = [jnp.maximum(xs[r], pltpu.roll(xs[15 - r], shift, 0)) for r in range(16)]
        for i, j in _BITONIC16:
            hi, lo = jnp.maximum(ys[i], ys[j]), jnp.minimum(ys[i], ys[j])
            ys[i], ys[j] = hi, lo
        xs = ys
    return xs


def _dup_bf16_words(x):
    bits = lax.bitcast_convert_type(x.astype(BF16).astype(F32), jnp.uint32)
    return bits | (bits >> 16)


def _peer_kernel(h2_ref, y_ref, mod_ref, wq_ref, keys_ref, u_ref, vt_ref, fg_ref, o_ref,
                 s0_ref, sv1_ref, r1b_ref, e1b_ref, n0w_ref, ccw_ref, pta_ref, ptb_ref, gt_ref, acc_ref, h2t_ref,
                 *, final, sub):
    bb, tb, d = y_ref.shape
    tm = bb * tb
    ec = u_ref.shape[1]
    cidx = pl.program_id(1)
    nt_dims = (((1,), (1,)), ((), ()))
    ngrp = PEER_NKEYS // 8

    @pl.when(cidx == 0)
    def _():
        q = jnp.dot(h2_ref[...], wq_ref[0], preferred_element_type=F32)
        sub8 = lax.broadcasted_iota(jnp.int32, (8, tm), 0)
        tops = []
        for p in range(2):
            stack = [jnp.zeros((8, tm), F32) for _ in range(PEER_TOPK)]
            for h in range(PEER_HEADS):
                col = (h * 2 + p) * PEER_NKEYS
                qs = q[:, col:col + PEER_NKEYS].astype(BF16)
                st = lax.dot_general(keys_ref[0, h, p], qs, nt_dims, preferred_element_type=F32)
                srt = _top16_sorted(st)
                if p == 0:
                    s0_ref[h] = st
                else:
                    st3 = st.reshape(ngrp, 8, tm)
                    rank = jnp.zeros((ngrp, 8, tm), F32)
                    for r in range(PEER_TOPK):
                        sv1_ref[h, r] = srt[r]
                        rank = rank + jnp.where(srt[r][None] > st3, 1.0, 0.0)
                    r1b_ref[h] = rank.reshape(PEER_NKEYS, tm).astype(BF16)
                    e1b_ref[h] = jnp.exp(st3 - srt[0][None]).reshape(PEER_NKEYS, tm).astype(BF16)
                stack = [jnp.where(sub8 == h, srt[r], stack[r]) for r in range(PEER_TOPK)]
            tops.append(stack)
        cands = [tops[0][a] + tops[1][b] for a, b in _CANDS]
        work = list(cands)
        theta = jnp.full((8, tm), NEG, F32)
        cnt = jnp.zeros((8, tm), F32)
        for _ in range(PEER_TOPK):
            mx = functools.reduce(jnp.maximum, work)
            eq = [w == mx for w in work]
            theta = jnp.where(cnt < PEER_TOPK, mx, theta)
            cnt = cnt + functools.reduce(jnp.add, [e.astype(F32) for e in eq])
            work = [jnp.where(e, NEG, w) for e, w in zip(eq, work)]
        top = tops[0][0] + tops[1][0]
        z = functools.reduce(jnp.add, [jnp.where(cd >= theta, jnp.exp(cd - top), 0.0) for cd in cands])
        inv_z = 1.0 / z
        rep = lambda row: jnp.broadcast_to(row, (8, tm))[None]
        for h in range(PEER_HEADS):
            s03 = s0_ref[h].reshape(ngrp, 8, tm)
            th8 = rep(theta[h:h + 1, :])
            n0 = jnp.zeros((ngrp, 8, tm), F32)
            for b in range(PEER_TOPK):
                n0 = n0 + jnp.where(s03 + sv1_ref[h, b][None] >= th8, 1.0, 0.0)
            cc = jnp.exp(s03 - rep(tops[0][0][h:h + 1, :])) * rep(inv_z[h:h + 1, :])
            n0w = _dup_bf16_words(n0.reshape(PEER_NKEYS, tm))
            ccw = _dup_bf16_words(cc.reshape(PEER_NKEYS, tm))
            for l in range(tm // LANES):
                lanes = slice(l * LANES, (l + 1) * LANES)
                n0w_ref[h, l, 0:PEER_NKEYS, :] = n0w[:, lanes]
                ccw_ref[h, l, 0:PEER_NKEYS, :] = ccw[:, lanes]
        acc_ref[...] = jnp.zeros_like(acc_ref)
        ptb_ref[...] = jnp.zeros_like(ptb_ref)
        h2t_ref[...] = h2_ref[...].astype(F32).T.astype(BF16)

    groups = ec // PEER_NKEYS
    nsub = PEER_NKEYS // sub
    prev = jnp.maximum(cidx - 1, 0)

    def step(pt_next, pt_cur):
        pt_next[...] = jnp.dot(u_ref[0], h2t_ref[...], preferred_element_type=F32)
        def row(ref, h, i):
            return jnp.concatenate([pltpu.bitcast(ref[h, l, pl.ds(i, sub // 2, stride=0), :], BF16)
                                    for l in range(tm // LANES)], axis=1)

        def group(g, carry):
            i = prev * groups + g
            for jb in range(nsub):
                j0 = jb * sub
                r0 = g * PEER_NKEYS + j0
                pre = pt_cur[pl.ds(r0, sub), :]
                act = (0.5 * pre * (1.0 + lax.erf(pre * 0.7071067811865476))).astype(BF16)
                wgt = jnp.zeros((sub, tm), BF16)
                for h in range(PEER_HEADS):
                    val = e1b_ref[h, j0:j0 + sub, :] * row(ccw_ref, h, i)
                    wgt = wgt + jnp.where(r1b_ref[h, j0:j0 + sub, :] < row(n0w_ref, h, i), val, jnp.zeros_like(val))
                gt_ref[pl.ds(r0, sub), :] = act * wgt
            return carry

        for g in range(groups):
            group(g, 0)
        acc_ref[...] += jnp.dot(vt_ref[0], gt_ref[...], preferred_element_type=F32)

    @pl.when(cidx % 2 == 0)
    def _():
        step(pta_ref, ptb_ref)

    @pl.when(cidx % 2 == 1)
    def _():
        step(ptb_ref, pta_ref)

    @pl.when(cidx == pl.num_programs(1) - 1)
    def _():
        peer = acc_ref[...].T.reshape(bb, tb, d)
        y = y_ref[...] + mod_ref[...][:, :, 5 * d:6 * d] * peer
        if final:
            y = _rms(y, fg_ref[...])
        o_ref[...] = y


def _peer(h2, y, mod, wts, final_g, layer, bb, tb, ec, final):
    bn, t, d = y.shape
    nt = t // tb
    tm = bb * tb
    n_exp = wts["peer_u"].shape[1]
    nq = wts["w_peer_q"].shape[-1]
    sub = 32
    assert tm % LANES == 0
    tile = lambda dt: pltpu.VMEM((PEER_HEADS, PEER_NKEYS, tm), dt)
    row_tile = lambda: pltpu.VMEM((PEER_HEADS, tm // LANES, PEER_NKEYS + 8, LANES), jnp.uint32)
    nc = n_exp // ec
    return pl.pallas_call(
        functools.partial(_peer_kernel, final=final, sub=sub),
        grid=(bn * t // tm, nc + 1),
        in_specs=[pl.BlockSpec((tm, d), lambda i, c: (i, 0)),
                  pl.BlockSpec((bb, tb, d), lambda i, c: (i // nt, i % nt, 0)),
                  pl.BlockSpec((bb, 1, 6 * d), lambda i, c: (i // nt, 0, 0)),
                  _const_spec((1, d, nq), lambda i, c: (layer, 0, 0)),
                  _const_spec((1, PEER_HEADS, 2, PEER_NKEYS, PEER_NKEYS), lambda i, c: (layer, 0, 0, 0, 0)),
                  pl.BlockSpec((1, ec, d), lambda i, c: (layer, jnp.minimum(c, nc - 1), 0)),
                  pl.BlockSpec((1, d, ec), lambda i, c: (layer, 0, jnp.maximum(c - 1, 0))),
                  _const_spec((1, 1, d), lambda i, c: (0, 0, 0))],
        out_specs=pl.BlockSpec((bb, tb, d), lambda i, c: (i // nt, i % nt, 0)),
        out_shape=jax.ShapeDtypeStruct((bn, t, d), F32),
        scratch_shapes=[tile(F32),
                        pltpu.VMEM((PEER_HEADS, PEER_TOPK, 8, tm), F32),
                        tile(BF16), tile(BF16), row_tile(), row_tile(),
                        pltpu.VMEM((ec, tm), F32),
                        pltpu.VMEM((ec, tm), F32),
                        pltpu.VMEM((ec, tm), BF16),
                        pltpu.VMEM((d, tm), F32),
                        pltpu.VMEM((d, tm), BF16)],
        compiler_params=_params(("arbitrary", "arbitrary")),
        name="peer",
    )(h2, y, mod, wts["w_peer_q"], wts["peer_keys"], wts["peer_u"], wts["peer_vt"], final_g)


def _prepare_weights(w_in, w_gla_f2, b_gla_f, gla_norm_g, b_ml_i, b_ml_f, ml_norm_g, conv_w, conv_b, w_rg_a, b_rg_a,
                     w_rg_x, b_rg_x, rg_lambda, w_br_gla, w_br_ml, w_br_rg, w_out, w_peer_q, peer_keys, peer_u, peer_v,
                     norm1_g, norm2_g):
    n_layers, d, _ = w_in.shape
    offs, o = {}, 0
    for name, w in (("gq", 512), ("gk", 512), ("gv", 1024), ("gg", 1024), ("gf", GLA_RANK),
                    ("mq", 512), ("mk", 512), ("mv", 1024), ("mo", 1024), ("mi", N_HEADS), ("mf", N_HEADS),
                    ("rx", 1024), ("mg", 3072)):
        offs[name] = (o, o + w)
        o += w
    assert o == w_in.shape[-1]
    cut = lambda name: w_in[:, :, offs[name][0]:offs[name][1]]
    w_main = jnp.concatenate([cut(n) for n, _ in IN_PIECES], axis=-1).astype(BF16)
    n_small = GLA_RANK + 2 * N_HEADS
    w_small = jnp.concatenate([cut("gf"), cut("mi"), cut("mf"),
                               jnp.zeros((n_layers, d, SMALL_W - n_small), F32)], axis=-1).astype(BF16)
    wf2 = jnp.concatenate([w_gla_f2, jnp.zeros((n_layers, SMALL_W - GLA_RANK, w_gla_f2.shape[-1]), F32)],
                          axis=1).astype(BF16)
    bias_small = jnp.concatenate([jnp.zeros((n_layers, GLA_RANK), F32), b_ml_i, b_ml_f,
                                  jnp.zeros((n_layers, SMALL_W - n_small), F32)], axis=-1)
    row = lambda a: a.reshape(n_layers, 1, a.shape[-1])
    return dict(
        w_main=w_main, w_small=w_small, wf2=wf2, b_gla_f=row(b_gla_f), gla_norm_g=row(gla_norm_g),
        bias_small=row(bias_small), ml_norm_g=row(ml_norm_g),
        conv_w=conv_w, conv_b=row(conv_b), w_rg_a=w_rg_a.astype(BF16), b_rg_a=row(b_rg_a),
        w_rg_x=w_rg_x.astype(BF16), b_rg_x=row(b_rg_x), rg_lambda=row(rg_lambda),
        w_br_gla=w_br_gla.astype(BF16), w_br_ml=w_br_ml.astype(BF16), w_br_rg=w_br_rg.astype(BF16),
        w_out=w_out.astype(BF16), w_peer_q=w_peer_q.astype(BF16), peer_keys=peer_keys.astype(BF16),
        peer_u=peer_u.astype(BF16), peer_vt=jnp.swapaxes(peer_v, 1, 2).astype(BF16),
        norm1_g=row(norm1_g), norm2_g=row(norm2_g))


def _layer(x, mod, state, wts, final_g, layer, final, cfg):
    bn, t, d = x.shape
    fresh = state is None
    act_dtype = BF16 if fresh else F32
    outs = _in_projection(x, mod, wts["norm1_g"], wts["w_main"], wts["w_small"], layer,
                          cfg["bb"], cfg["tb"], act_dtype, rx_time_major=fresh)
    p = {name: o for (name, _), o in zip(IN_PIECES, outs[:-1])}
    small = outs[-1]
    if fresh:
        s_gla = ml_state = None
        rx = p["rx"].reshape(t * bn, d)
        conv_buf = h0 = None
    else:
        s_gla, c0, n0, m0, h0, conv_buf = state
        ml_state = (c0, n0, m0)
        rx = jnp.swapaxes(p["rx"].reshape(bn, t, d), 0, 1).reshape(t * bn, d)
    o_gla, s_gla_new = _gla_mixer(p, small, wts["wf2"], wts["b_gla_f"], wts["gla_norm_g"], s_gla, layer,
                                  bn, t, cfg["gla_bb"], act_dtype)
    o_ml, c_new, n_new, m_new = _mlstm_mixer(p, small, wts["bias_small"], wts["ml_norm_g"], ml_state, layer,
                                             bn, t, cfg["ml_bb"], act_dtype)
    hs, h_fin, buf_new = _rglru(rx, conv_buf, h0, wts, layer, bn, t, cfg["rg_tc"], act_dtype)
    if fresh:
        o_rg = hs.reshape(t, bn * d)
    else:
        o_rg = jnp.swapaxes(hs.reshape(t, bn, d), 0, 1).reshape(bn * t, d)
    y, h2 = _merge(o_gla, o_ml, o_rg, p["mg"], x, mod, wts["norm2_g"], wts, layer, cfg["bb"], cfg["tb"],
                   rg_time_major=fresh)
    out = _peer(h2, y, mod, wts, final_g, layer, cfg["peer_bb"], cfg["peer_tb"], cfg["peer_ec"], final)
    new_state = (s_gla_new, c_new, n_new, m_new[:, 0, :N_HEADS], h_fin,
                 jnp.swapaxes(buf_new.reshape(CONV_W - 1, bn, d), 0, 1))
    return out, new_state


def kernel(x_prompt, x_sample, state_gla, state_mlstm_c, state_mlstm_n, state_mlstm_m, state_rglru_h, state_rglru_conv, c_prompt, c_sample, w_ada, b_ada, norm1_g, norm2_g, w_in, w_gla_f2, b_gla_f, gla_norm_g, b_ml_i, b_ml_f, ml_norm_g, conv_w, conv_b, w_rg_a, b_rg_a, w_rg_x, b_rg_x, rg_lambda, w_br_gla, w_br_ml, w_br_rg, w_out, w_peer_q, peer_keys, peer_u, peer_v, final_norm_g):
    n_layers = w_in.shape[0]
    bp, tp, d = x_prompt.shape
    bs, ts, _ = x_sample.shape
    wts = _prepare_weights(w_in, w_gla_f2, b_gla_f, gla_norm_g, b_ml_i, b_ml_f, ml_norm_g, conv_w, conv_b, w_rg_a,
                           b_rg_a, w_rg_x, b_rg_x, rg_lambda, w_br_gla, w_br_ml, w_br_rg, w_out, w_peer_q,
                           peer_keys, peer_u, peer_v, norm1_g, norm2_g)
    final_g = final_norm_g.reshape(1, 1, d)
    mod_all = _modulation(jnp.concatenate([c_prompt, c_sample], axis=0), w_ada, b_ada)
    m0_pad = jnp.pad(state_mlstm_m, ((0, 0), (0, 0), (0, SMALL_W - N_HEADS))).reshape(n_layers, bs, 1, SMALL_W)
    conv_tm = jnp.swapaxes(state_rglru_conv, 1, 2).reshape(n_layers, (CONV_W - 1) * bs, d)
    sample_state = (state_gla, state_mlstm_c, state_mlstm_n, m0_pad, state_rglru_h, conv_tm)

    cfg_p = dict(bb=1, tb=min(256, tp), gla_bb=min(4, bp), ml_bb=min(2, bp), rg_tc=min(128, tp), peer_bb=1,
                 peer_tb=min(512, tp), peer_ec=1024)
    cfg_s = dict(bb=min(16, bs), tb=ts, gla_bb=min(4, bs), ml_bb=min(4, bs), rg_tc=ts, peer_bb=min(512 // ts, bs),
                 peer_tb=ts, peer_ec=1024)
    yp, ys = x_prompt, x_sample
    new_p, new_s = [], []
    for l in range(n_layers):
        final = l == n_layers - 1
        mod_p = mod_all[l, :bp].reshape(bp, 1, 6 * d)
        mod_s = mod_all[l, bp:].reshape(bs, 1, 6 * d)
        yp, st_p = _layer(yp, mod_p, None, wts, final_g, l, final, cfg_p)
        ys, st_s = _layer(ys, mod_s, sample_state, wts, final_g, l, final, cfg_s)
        new_p.append(st_p)
        new_s.append(st_s)
    stacked_p = [jnp.stack(z) for z in zip(*new_p)]
    stacked_s = [jnp.stack(z) for z in zip(*new_s)]
    return (yp, ys, *stacked_p, *stacked_s)
```

```python
import functools

import jax
import jax.numpy as jnp
from jax import lax
from jax.experimental import pallas as pl
from jax.experimental.pallas import tpu as pltpu

F32 = jnp.float32
BF16 = jnp.bfloat16
HIGHEST = lax.Precision.HIGHEST

EPS = 1e-6
N_HEADS = 4
DK = 128
DV = 256
GLA_RANK = 16
GLA_GATE_TEMP = 16.0
CHUNK = 64
RG_BLOCKS = 8
RG_C = 8.0
CONV_W = 4
PEER_HEADS = 8
PEER_NKEYS = 128
PEER_TOPK = 16
NEG = -1e30
LANES = 128

VMEM_LIMIT_BYTES = 56 * 1024 * 1024

IN_PIECES = (("gq", 512), ("gk", 512), ("gv", 1024), ("gg", 1024),
             ("mq", 512), ("mk", 512), ("mv", 1024), ("mo", 1024),
             ("rx", 1024), ("mg", 3072))
SMALL_W = 128
SM_I = GLA_RANK
SM_F = GLA_RANK + N_HEADS


def _params(sem):
    return pltpu.CompilerParams(dimension_semantics=sem, vmem_limit_bytes=VMEM_LIMIT_BYTES)


def _const_spec(shape, index_map):
    return pl.BlockSpec(shape, index_map, pipeline_mode=pl.Buffered(1))


def _log_sigmoid(x):
    return jnp.minimum(x, 0.0) - jnp.log1p(jnp.exp(-jnp.abs(x)))


def _softplus(x):
    return jnp.maximum(x, 0.0) + jnp.log1p(jnp.exp(-jnp.abs(x)))


def _sigmoid(x):
    return 1.0 / (1.0 + jnp.exp(-x))


def _rms(x, g):
    return x * lax.rsqrt(jnp.mean(x * x, axis=-1, keepdims=True) + EPS) * g


def _pad_rows(a, rows, value=0.0):
    if a.shape[0] == rows:
        return a
    return jnp.concatenate([a, jnp.full((rows - a.shape[0], a.shape[1]), value, a.dtype)], axis=0)


def _row_to_col(row, eye):
    n = row.shape[1]
    return jnp.sum(jnp.where(eye, jnp.broadcast_to(row, (n, n)), 0.0), axis=1, keepdims=True)


def _col_to_row(col, eye):
    n = col.shape[0]
    return jnp.sum(jnp.where(eye, jnp.broadcast_to(col, (n, n)), 0.0), axis=0, keepdims=True)


def _mod_kernel(c_ref, w_ref, b_ref, o_ref):
    c = c_ref[...]
    s = (c * _sigmoid(c)).astype(BF16)
    o_ref[0] = jnp.dot(s, w_ref[0].astype(BF16), preferred_element_type=F32) + b_ref[0]


def _modulation(c_all, w_ada, b_ada):
    n_layers, d, n6 = w_ada.shape
    rows = c_all.shape[0]
    tn = 1536
    return pl.pallas_call(
        _mod_kernel,
        grid=(n_layers, n6 // tn),
        in_specs=[pl.BlockSpec((rows, d), lambda l, n: (0, 0)),
                  pl.BlockSpec((1, d, tn), lambda l, n: (l, 0, n)),
                  pl.BlockSpec((1, 1, tn), lambda l, n: (l, 0, n))],
        out_specs=pl.BlockSpec((1, rows, tn), lambda l, n: (l, 0, n)),
        out_shape=jax.ShapeDtypeStruct((n_layers, rows, n6), F32),
        compiler_params=_params(("arbitrary", "arbitrary")),
        name="adaln_mod",
    )(c_all, w_ada, b_ada.reshape(n_layers, 1, n6))


def _inproj_kernel(x_ref, mod_ref, g_ref, wm_ref, ws_ref, *out_refs):
    bb, tb, d = x_ref.shape
    mod = mod_ref[...]
    h = _rms(x_ref[...], g_ref[...]) * (1.0 + mod[:, :, d:2 * d]) + mod[:, :, 0:d]
    hb = h.reshape(bb * tb, d).astype(BF16)
    off = 0
    for r, (_, w) in zip(out_refs[:-1], IN_PIECES):
        r[...] = jnp.dot(hb, wm_ref[0, :, off:off + w], preferred_element_type=F32).astype(r.dtype)
        off += w
    out_refs[-1][...] = jnp.dot(hb, ws_ref[0], preferred_element_type=F32)


def _in_projection(x, mod, g1, w_main, w_small, layer, bb, tb, out_dtype, rx_time_major):
    bn, t, d = x.shape
    nt = t // tb
    rows = bb * tb
    n_main = w_main.shape[-1]
    out_shapes, out_specs = [], []
    for name, w in IN_PIECES:
        if name == "rx" and rx_time_major:
            out_shapes.append(jax.ShapeDtypeStruct((t, bn * w), out_dtype))
            out_specs.append(pl.BlockSpec((tb, w), lambda i, j: (j, i)))
        else:
            out_shapes.append(jax.ShapeDtypeStruct((bn * t, w), out_dtype))
            out_specs.append(pl.BlockSpec((rows, w), lambda i, j: (i * nt + j, 0)))
    out_shapes.append(jax.ShapeDtypeStruct((bn * t, SMALL_W), F32))
    out_specs.append(pl.BlockSpec((rows, SMALL_W), lambda i, j: (i * nt + j, 0)))
    if rx_time_major:
        assert bb == 1
    return pl.pallas_call(
        _inproj_kernel,
        grid=(bn // bb, nt),
        in_specs=[pl.BlockSpec((bb, tb, d), lambda i, j: (i, j, 0)),
                  pl.BlockSpec((bb, 1, 6 * d), lambda i, j: (i, 0, 0)),
                  _const_spec((1, 1, d), lambda i, j: (layer, 0, 0)),
                  _const_spec((1, d, n_main), lambda i, j: (layer, 0, 0)),
                  _const_spec((1, d, SMALL_W), lambda i, j: (layer, 0, 0))],
        out_specs=out_specs,
        out_shape=out_shapes,
        compiler_params=_params(("arbitrary", "arbitrary")),
        name="in_proj",
    )(x, mod, g1, w_main, w_small)


def _gla_kernel(*refs, bb, tc, has_state):
    if has_state:
        gq, gk, gv, gg, sm, wf2, bf, gn, s0, o_ref, s_ref = refs
    else:
        gq, gk, gv, gg, sm, wf2, bf, gn, o_ref, s_ref = refs
        s0 = None
    c = CHUNK

    @pl.when(pl.program_id(1) == 0)
    def _():
        if has_state:
            s_ref[...] = s0[...]
        else:
            s_ref[...] = jnp.zeros_like(s_ref)

    ri = lax.broadcasted_iota(jnp.int32, (c, c), 0)
    ci = lax.broadcasted_iota(jnp.int32, (c, c), 1)
    causal = ri >= ci
    tril = causal.astype(F32)
    hk = N_HEADS * DK
    valid = lax.broadcasted_iota(jnp.int32, (c, hk), 0) < tc
    eye = (lax.broadcasted_iota(jnp.int32, (DK, DK), 0) == lax.broadcasted_iota(jnp.int32, (DK, DK), 1))
    nt_dims = (((1,), (1,)), ((), ()))
    tn_dims = (((0,), (0,)), ((), ()))
    for s in range(bb):
        smb = _pad_rows(sm[s], c).astype(BF16)
        x = jnp.dot(smb, wf2[0], preferred_element_type=F32) + bf[0]
        logf = jnp.where(valid, _log_sigmoid(x) * (1.0 / GLA_GATE_TEMP), 0.0)
        b = jnp.dot(tril, logf, precision=HIGHEST, preferred_element_type=F32)
        b_last = b[c - 1:c, :]
        q = _pad_rows(gq[s].astype(F32), c) * (DK ** -0.5)
        k = _pad_rows(gk[s].astype(F32), c)
        qd = (q * jnp.exp(b)).astype(BF16)
        kd = (k * jnp.exp(-b)).astype(BF16)
        kend = (k * jnp.exp(b_last - b)).astype(BF16)
        vb = _pad_rows(gv[s].astype(F32), c).astype(BF16)
        g = gg[s].astype(F32)
        g = g * _sigmoid(g)
        for h in range(N_HEADS):
            kcols = slice(h * DK, (h + 1) * DK)
            vcols = slice(h * DV, (h + 1) * DV)
            att = lax.dot_general(qd[:, kcols], kd[:, kcols], nt_dims, preferred_element_type=F32)
            att = jnp.where(causal, att, 0.0).astype(BF16)
            st = s_ref[s, h]
            o = (jnp.dot(att, vb[:, vcols], preferred_element_type=F32)
                 + jnp.dot(qd[:, kcols], st.astype(BF16), preferred_element_type=F32))
            dec = jnp.exp(_row_to_col(b_last[:, kcols], eye))
            s_ref[s, h] = dec * st + lax.dot_general(kend[:, kcols], vb[:, vcols], tn_dims,
                                                     preferred_element_type=F32)
            o_ref[s, :, vcols] = (_rms(o[:tc], gn[0]) * g[:, vcols]).astype(o_ref.dtype)


def _gla_mixer(p, small, wf2, bfv, gn, s0, layer, bn, t, bb, out_dtype):
    tc = min(CHUNK, t)
    nch = t // tc
    assert tc * nch == t
    rspec = lambda w: pl.BlockSpec((bb, tc, w), lambda i, j: (i, j, 0))
    r3 = lambda a: a.reshape(bn, t, a.shape[-1])
    in_specs = [rspec(512), rspec(512), rspec(1024), rspec(1024), rspec(SMALL_W),
                _const_spec((1, SMALL_W, 512), lambda i, j: (layer, 0, 0)),
                _const_spec((1, 1, 512), lambda i, j: (layer, 0, 0)),
                _const_spec((1, 1, DV), lambda i, j: (layer, 0, 0))]
    args = [r3(p["gq"]), r3(p["gk"]), r3(p["gv"]), r3(p["gg"]), r3(small), wf2, bfv, gn]
    sspec = pl.BlockSpec((bb, N_HEADS, DK, DV), lambda i, j: (i, 0, 0, 0))
    if s0 is not None:
        in_specs.append(pl.BlockSpec((None, bb, N_HEADS, DK, DV), lambda i, j: (layer, i, 0, 0, 0)))
        args.append(s0)
    o, s_new = pl.pallas_call(
        functools.partial(_gla_kernel, bb=bb, tc=tc, has_state=s0 is not None),
        grid=(bn // bb, nch),
        in_specs=in_specs,
        out_specs=[rspec(1024), sspec],
        out_shape=[jax.ShapeDtypeStruct((bn, t, 1024), out_dtype),
                   jax.ShapeDtypeStruct((bn, N_HEADS, DK, DV), F32)],
        compiler_params=_params(("arbitrary", "arbitrary")),
        name="gla_mixer",
    )(*args)
    return o.reshape(bn * t, 1024), s_new


def _mlstm_kernel(*refs, bb, tc, has_state):
    if has_state:
        mq, mk, mv, mo, sm, bias, gn, c0, n0, m0, o_ref, c_ref, n_ref, m_ref = refs
    else:
        mq, mk, mv, mo, sm, bias, gn, o_ref, c_ref, n_ref, m_ref = refs
    c = CHUNK

    @pl.when(pl.program_id(1) == 0)
    def _():
        if has_state:
            c_ref[...] = c0[...]
            n_ref[...] = n0[...]
            m_ref[...] = m0[...]
        else:
            c_ref[...] = jnp.zeros_like(c_ref)
            n_ref[...] = jnp.zeros_like(n_ref)
            m_ref[...] = jnp.zeros_like(m_ref)

    ri = lax.broadcasted_iota(jnp.int32, (c, c), 0)
    ci = lax.broadcasted_iota(jnp.int32, (c, c), 1)
    causal = ri >= ci
    eye = ri == ci
    tril = causal.astype(F32)
    trow = lax.broadcasted_iota(jnp.int32, (c, SMALL_W), 0)
    valid = trow < tc
    nt_dims = (((1,), (1,)), ((), ()))
    tn_dims = (((0,), (0,)), ((), ()))
    for s in range(bb):
        t1 = _pad_rows(sm[s], c) + bias[0]
        ig_t = jnp.where(valid, t1, NEG)
        lf_t = jnp.where(valid, _log_sigmoid(t1), 0.0)
        f_t = jnp.dot(tril, lf_t, precision=HIGHEST, preferred_element_type=F32)
        fa_t = pltpu.roll(f_t, SMALL_W - (SM_F - SM_I), 1)
        a_t = ig_t - fa_t
        cm_t = a_t
        for sh in (1, 2, 4, 8, 16, 32):
            cm_t = jnp.maximum(cm_t, jnp.where(trow >= sh, pltpu.roll(cm_t, sh, 0), NEG))
        m_prev = m_ref[s]
        m_t = fa_t + jnp.maximum(m_prev, cm_t)
        wi_t = jnp.exp(fa_t + m_prev - m_t)
        fm_t = fa_t - m_t
        em_t = jnp.exp(-m_t)
        m_last_row = m_t[c - 1:c, :]
        f_last_row = fa_t[c - 1:c, :]
        ws_t = jnp.exp(ig_t + f_last_row - fa_t - m_last_row)
        carry_row = jnp.exp(f_last_row + m_prev - m_last_row)
        qb_all = _pad_rows(mq[s].astype(F32), c).astype(BF16)
        k_all = _pad_rows(mk[s].astype(F32), c) * (DK ** -0.5)
        vb_all = _pad_rows(mv[s].astype(F32), c).astype(BF16)
        g_all = _sigmoid(mo[s].astype(F32))
        for h in range(N_HEADS):
            kcols = slice(h * DK, (h + 1) * DK)
            vcols = slice(h * DV, (h + 1) * DV)
            qb = qb_all[:, kcols]
            q = qb.astype(F32)
            k = k_all[:, kcols]
            vb = vb_all[:, vcols]
            col = slice(SM_I + h, SM_I + h + 1)
            a_row = _col_to_row(a_t[:, col], eye)
            w_inter = wi_t[:, col]
            dmat = jnp.exp(jnp.where(causal, fm_t[:, col] + a_row, NEG))
            qk = lax.dot_general(qb, k.astype(BF16), nt_dims, preferred_element_type=F32) * dmat
            cs = c_ref[s, h]
            ns = n_ref[s, h:h + 1, :]
            num = (jnp.dot(qk.astype(BF16), vb, preferred_element_type=F32)
                   + w_inter * jnp.dot(qb, cs.astype(BF16), preferred_element_type=F32))
            den = jnp.sum(qk, axis=1, keepdims=True) + w_inter * jnp.sum(q * ns, axis=1, keepdims=True)
            hh = num / jnp.maximum(jnp.abs(den), em_t[:, col])
            carry = carry_row[:, col]
            kw = k * ws_t[:, col]
            c_ref[s, h] = carry * cs + lax.dot_general(kw.astype(BF16), vb, tn_dims, preferred_element_type=F32)
            n_ref[s, h:h + 1, :] = carry * ns + jnp.sum(kw, axis=0, keepdims=True)
            o_ref[s, :, vcols] = (_rms(hh[:tc], gn[0]) * g_all[:, vcols]).astype(o_ref.dtype)
        m_ref[s] = m_last_row


def _mlstm_mixer(p, small, bias, gn, state, layer, bn, t, bb, out_dtype):
    tc = min(CHUNK, t)
    nch = t // tc
    assert tc * nch == t
    rspec = lambda w: pl.BlockSpec((bb, tc, w), lambda i, j: (i, j, 0))
    r3 = lambda a: a.reshape(bn, t, a.shape[-1])
    in_specs = [rspec(512), rspec(512), rspec(1024), rspec(1024), rspec(SMALL_W),
                _const_spec((1, 1, SMALL_W), lambda i, j: (layer, 0, 0)),
                _const_spec((1, 1, DV), lambda i, j: (layer, 0, 0))]
    args = [r3(p["mq"]), r3(p["mk"]), r3(p["mv"]), r3(p["mo"]), r3(small), bias, gn]
    cspec = pl.BlockSpec((bb, N_HEADS, DK, DV), lambda i, j: (i, 0, 0, 0))
    nspec = pl.BlockSpec((bb, N_HEADS, DK), lambda i, j: (i, 0, 0))
    mspec = pl.BlockSpec((bb, 1, SMALL_W), lambda i, j: (i, 0, 0))
    if state is not None:
        c0, n0, m0 = state
        in_specs += [pl.BlockSpec((None, bb, N_HEADS, DK, DV), lambda i, j: (layer, i, 0, 0, 0)),
                     pl.BlockSpec((None, bb, N_HEADS, DK), lambda i, j: (layer, i, 0, 0)),
                     pl.BlockSpec((None, bb, 1, SMALL_W), lambda i, j: (layer, i, 0, 0))]
        args += [c0, n0, m0]
    o, c_new, n_new, m_new = pl.pallas_call(
        functools.partial(_mlstm_kernel, bb=bb, tc=tc, has_state=state is not None),
        grid=(bn // bb, nch),
        in_specs=in_specs,
        out_specs=[rspec(1024), cspec, nspec, mspec],
        out_shape=[jax.ShapeDtypeStruct((bn, t, 1024), out_dtype),
                   jax.ShapeDtypeStruct((bn, N_HEADS, DK, DV), F32),
                   jax.ShapeDtypeStruct((bn, N_HEADS, DK), F32),
                   jax.ShapeDtypeStruct((bn, 1, SMALL_W), F32)],
        compiler_params=_params(("arbitrary", "arbitrary")),
        name="mlstm_mixer",
    )(*args)
    return o.reshape(bn * t, 1024), c_new, n_new, m_new


def _rglru_kernel(*refs, nb, tc, has_state):
    if has_state:
        rx, cb, h0, cw, cbias, wa, ba, wx, bx, lam, hs_ref, hf_ref, nb_ref, xpad, a_s, u_s = refs
    else:
        rx, cw, cbias, wa, ba, wx, bx, lam, hs_ref, hf_ref, nb_ref, xpad, a_s, u_s = refs
    hist = (CONV_W - 1) * nb
    rows = tc * nb
    w = rx.shape[1]
    bw = w // RG_BLOCKS

    @pl.when(pl.program_id(0) == 0)
    def _():
        if has_state:
            xpad[0:hist, :] = cb[0]
            hf_ref[...] = h0[0]
        else:
            xpad[0:hist, :] = jnp.zeros((hist, w), F32)
            hf_ref[...] = jnp.zeros_like(hf_ref)

    xpad[hist:hist + rows, :] = rx[...].astype(F32)
    xc = cbias[0]
    for j in range(CONV_W):
        xc = xc + xpad[j * nb:j * nb + rows, :] * cw[0, j:j + 1, :]
    sp = _softplus(-lam[0])
    for n in range(RG_BLOCKS):
        cols = slice(n * bw, (n + 1) * bw)
        xn = xc[:, cols]
        xg = xn.astype(BF16)
        r = _sigmoid(jnp.dot(xg, wa[0, n], preferred_element_type=F32) + ba[0, :, cols])
        i = _sigmoid(jnp.dot(xg, wx[0, n], preferred_element_type=F32) + bx[0, :, cols])
        log_a = -RG_C * r * sp[:, cols]
        a_s[:, cols] = jnp.exp(log_a)
        u_s[:, cols] = jnp.sqrt(1.0 - jnp.exp(2.0 * log_a)) * (i * xn)

    def body(t, h):
        o = pl.multiple_of(t * nb, nb)
        h = a_s[pl.ds(o, nb), :] * h + u_s[pl.ds(o, nb), :]
        hs_ref[pl.ds(o, nb), :] = h.astype(hs_ref.dtype)
        return h

    hf_ref[...] = lax.fori_loop(0, tc, body, hf_ref[...])
    tail = xpad[rows:rows + hist, :]
    xpad[0:hist, :] = tail
    nb_ref[...] = tail


def _rglru(rx, conv_buf, h0, wts, layer, nb, t, tc, out_dtype):
    w = rx.shape[1]
    rows = tc * nb
    hist = (CONV_W - 1) * nb
    bw = w // RG_BLOCKS
    lsp = lambda shape: _const_spec((1,) + shape, lambda j: (layer,) + (0,) * len(shape))
    in_specs = [pl.BlockSpec((rows, w), lambda j: (j, 0))]
    args = [rx]
    if conv_buf is not None:
        in_specs += [lsp((hist, w)), lsp((nb, w))]
        args += [conv_buf, h0]
    in_specs += [lsp((CONV_W, w)), lsp((1, w)), lsp((RG_BLOCKS, bw, bw)), lsp((1, w)),
                 lsp((RG_BLOCKS, bw, bw)), lsp((1, w)), lsp((1, w))]
    args += [wts["conv_w"], wts["conv_b"], wts["w_rg_a"], wts["b_rg_a"], wts["w_rg_x"], wts["b_rg_x"],
             wts["rg_lambda"]]
    return pl.pallas_call(
        functools.partial(_rglru_kernel, nb=nb, tc=tc, has_state=conv_buf is not None),
        grid=(t // tc,),
        in_specs=in_specs,
        out_specs=[pl.BlockSpec((rows, w), lambda j: (j, 0)),
                   pl.BlockSpec((nb, w), lambda j: (0, 0)),
                   pl.BlockSpec((hist, w), lambda j: (0, 0))],
        out_shape=[jax.ShapeDtypeStruct((t * nb, w), out_dtype),
                   jax.ShapeDtypeStruct((nb, w), F32),
                   jax.ShapeDtypeStruct((hist, w), F32)],
        scratch_shapes=[pltpu.VMEM((rows + hist, w), F32), pltpu.VMEM((rows, w), F32), pltpu.VMEM((rows, w), F32)],
        compiler_params=_params(("arbitrary",)),
        name="rglru",
    )(*args)


def _merge_kernel(og, om, orr, mg, x_ref, mod_ref, g2, wbg, wbm, wbr, wo, y_ref, h2_ref):
    bb, tb, d = x_ref.shape
    gates = mg[...].astype(F32)
    merged = (_sigmoid(gates[:, 0:d]) * jnp.dot(og[...].astype(BF16), wbg[0], preferred_element_type=F32)
              + _sigmoid(gates[:, d:2 * d]) * jnp.dot(om[...].astype(BF16), wbm[0], preferred_element_type=F32)
              + _sigmoid(gates[:, 2 * d:3 * d]) * jnp.dot(orr[...].astype(BF16), wbr[0], preferred_element_type=F32))
    z = jnp.dot(merged.astype(BF16), wo[0], preferred_element_type=F32)
    mod = mod_ref[...]
    y = x_ref[...] + mod[:, :, 2 * d:3 * d] * z.reshape(bb, tb, d)
    y_ref[...] = y
    h2 = _rms(y, g2[...]) * (1.0 + mod[:, :, 4 * d:5 * d]) + mod[:, :, 3 * d:4 * d]
    h2_ref[...] = h2.reshape(bb * tb, d).astype(BF16)


def _merge(og, om, orr, mg, x, mod, g2, wts, layer, bb, tb, rg_time_major):
    bn, t, d = x.shape
    nt = t // tb
    rows = bb * tb
    rspec = lambda w: pl.BlockSpec((rows, w), lambda i, j: (i * nt + j, 0))
    if rg_time_major:
        assert bb == 1
        rg_spec = pl.BlockSpec((tb, d), lambda i, j: (j, i))
    else:
        rg_spec = rspec(d)
    wspec = _const_spec((1, d, d), lambda i, j: (layer, 0, 0))
    return pl.pallas_call(
        _merge_kernel,
        grid=(bn // bb, nt),
        in_specs=[rspec(d), rspec(d), rg_spec, rspec(3 * d),
                  pl.BlockSpec((bb, tb, d), lambda i, j: (i, j, 0)),
                  pl.BlockSpec((bb, 1, 6 * d), lambda i, j: (i, 0, 0)),
                  _const_spec((1, 1, d), lambda i, j: (layer, 0, 0)),
                  wspec, wspec, wspec, wspec],
        out_specs=[pl.BlockSpec((bb, tb, d), lambda i, j: (i, j, 0)), rspec(d)],
        out_shape=[jax.ShapeDtypeStruct((bn, t, d), F32), jax.ShapeDtypeStruct((bn * t, d), BF16)],
        compiler_params=_params(("arbitrary", "arbitrary")),
        name="merge",
    )(og, om, orr, mg, x, mod, g2, wts["w_br_gla"], wts["w_br_ml"], wts["w_br_rg"], wts["w_out"])


def _sort16_pairs():
    n, pairs, p = 16, [], 1
    while p < n:
        k = p
        while k >= 1:
            for j in range(k % p, n - k, 2 * k):
                for i in range(min(k, n - j - k)):
                    if (i + j) // (2 * p) == (i + j + k) // (2 * p):
                        pairs.append((i + j, i + j + k))
            k //= 2
        p *= 2
    return pairs


_SORT16 = _sort16_pairs()
_BITONIC16 = [(i, i + dd) for dd in (8, 4, 2, 1) for i in range(16) if (i & dd) == 0]
_CANDS = [(a, b) for a in range(PEER_TOPK) for b in range(PEER_TOPK) if (a + 1) * (b + 1) <= PEER_TOPK]


def _top16_sorted(st):
    xs = [st[g * 8:(g + 1) * 8, :] for g in range(16)]
    for i, j in _SORT16:
        hi, lo = jnp.maximum(xs[i], xs[j]), jnp.minimum(xs[i], xs[j])
        xs[i], xs[j] = hi, lo
    for shift in (4, 2, 1):
        ys = [jnp.maximum(xs[r], pltpu.roll(xs[15 - r], shift, 0)) for r in range(16)]
        for i, j in _BITONIC16:
            hi, lo = jnp.maximum(ys[i], ys[j]), jnp.minimum(ys[i], ys[j])
            ys[i], ys[j] = hi, lo
        xs = ys
    return xs


def _dup_bf16_words(x):
    bits = lax.bitcast_convert_type(x.astype(BF16).astype(F32), jnp.uint32)
    return bits | (bits >> 16)


def _peer_kernel(h2_ref, y_ref, mod_ref, wq_ref, keys_ref, u_ref, vt_ref, fg_ref, o_ref,
                 s0_ref, s1_ref, sv1_ref, r1b_ref, e1b_ref, n0w_ref, ccw_ref, pta_ref, ptb_ref, gt_ref, acc_ref, h2t_ref,
                 *, final, sub):
    bb, tb, d = y_ref.shape
    tm = bb * tb
    ec = u_ref.shape[1]
    cidx = pl.program_id(1)
    nt_dims = (((1,), (1,)), ((), ()))
    ngrp = PEER_NKEYS // 8

    @pl.when(cidx == 0)
    def _():
        q = jnp.dot(h2_ref[...], wq_ref[0], preferred_element_type=F32)
        for p, s_ref in enumerate((s0_ref, s1_ref)):
            for h in range(PEER_HEADS):
                col = (h * 2 + p) * PEER_NKEYS
                qs = q[:, col:col + PEER_NKEYS].astype(BF16)
                s_ref[h] = lax.dot_general(keys_ref[0, h, p], qs, nt_dims, preferred_element_type=F32)
        sub8 = lax.broadcasted_iota(jnp.int32, (8, LANES), 0)
        rep = lambda row: jnp.broadcast_to(row, (8, LANES))[None]
        for l in range(tm // LANES):
            lanes = slice(l * LANES, (l + 1) * LANES)
            tops = []
            for p, s_ref in enumerate((s0_ref, s1_ref)):
                stack = [jnp.zeros((8, LANES), F32) for _ in range(PEER_TOPK)]
                for h in range(PEER_HEADS):
                    st = s_ref[h, :, lanes]
                    srt = _top16_sorted(st)
                    if p == 1:
                        st3 = st.reshape(ngrp, 8, LANES)
                        rank = jnp.full((ngrp, 8, LANES), float(PEER_TOPK), F32)
                        for r in reversed(range(PEER_TOPK)):
                            sv1_ref[h, r, :, lanes] = srt[r]
                            rank = jnp.where(srt[r][None] > st3, rank, float(r))
                        r1b_ref[h, :, lanes] = rank.reshape(PEER_NKEYS, LANES).astype(BF16)
                        e1b_ref[h, :, lanes] = jnp.exp(st3 - srt[0][None]).reshape(PEER_NKEYS, LANES).astype(BF16)
                    stack = [jnp.where(sub8 == h, srt[r], stack[r]) for r in range(PEER_TOPK)]
                tops.append(stack)
            work = [tops[0][a] + tops[1][b] for a, b in _CANDS]
            theta = jnp.full((8, LANES), NEG, F32)
            cnt = jnp.zeros((8, LANES), F32)
            for _ in range(PEER_TOPK):
                mx = functools.reduce(jnp.maximum, work)
                eq = [w == mx for w in work]
                theta = jnp.where(cnt < PEER_TOPK, mx, theta)
                cnt = cnt + functools.reduce(jnp.add, [e.astype(F32) for e in eq])
                work = [jnp.where(e, NEG, w) for e, w in zip(eq, work)]
            top = tops[0][0] + tops[1][0]
            z = jnp.zeros((8, LANES), F32)
            for a, b in _CANDS:
                cd = tops[0][a] + tops[1][b]
                z = z + jnp.where(cd >= theta, jnp.exp(cd - top), 0.0)
            inv_z = 1.0 / z
            for h in range(PEER_HEADS):
                s03 = s0_ref[h, :, lanes].reshape(ngrp, 8, LANES)
                th8 = rep(theta[h:h + 1, :])
                n0 = jnp.full((ngrp, 8, LANES), float(PEER_TOPK), F32)
                for b in reversed(range(PEER_TOPK)):
                    n0 = jnp.where(s03 + sv1_ref[h, b, :, lanes][None] >= th8, n0, float(b))
                cc = jnp.exp(s03 - rep(tops[0][0][h:h + 1, :])) * rep(inv_z[h:h + 1, :])
                n0w_ref[h, l, 0:PEER_NKEYS, :] = _dup_bf16_words(n0.reshape(PEER_NKEYS, LANES))
                ccw_ref[h, l, 0:PEER_NKEYS, :] = _dup_bf16_words(cc.reshape(PEER_NKEYS, LANES))
        acc_ref[...] = jnp.zeros_like(acc_ref)
        ptb_ref[...] = jnp.zeros_like(ptb_ref)
        h2t_ref[...] = h2_ref[...].astype(F32).T.astype(BF16)

    groups = ec // PEER_NKEYS
    nsub = PEER_NKEYS // sub
    prev = jnp.maximum(cidx - 1, 0)

    def step(pt_next, pt_cur):
        pt_next[...] = jnp.dot(u_ref[0], h2t_ref[...], preferred_element_type=F32)

        def row(ref, h, i):
            return jnp.concatenate([pltpu.bitcast(ref[h, l, pl.ds(i, sub // 2, stride=0), :], BF16)
                                    for l in range(tm // LANES)], axis=1)

        for g in range(groups):
            i = prev * groups + g
            for jb in range(nsub):
                j0 = jb * sub
                r0 = g * PEER_NKEYS + j0
                pre = pt_cur[r0:r0 + sub, :]
                act = (0.5 * pre * (1.0 + lax.erf(pre * 0.7071067811865476))).astype(BF16)
                wgt = jnp.zeros((sub, tm), BF16)
                for h in range(PEER_HEADS):
                    val = e1b_ref[h, j0:j0 + sub, :] * row(ccw_ref, h, i)
                    wgt = wgt + jnp.where(r1b_ref[h, j0:j0 + sub, :] < row(n0w_ref, h, i), val, jnp.zeros_like(val))
                gt_ref[r0:r0 + sub, :] = act * wgt
        acc_ref[...] += jnp.dot(vt_ref[0], gt_ref[...], preferred_element_type=F32)

    @pl.when(cidx % 2 == 0)
    def _():
        step(pta_ref, ptb_ref)

    @pl.when(cidx % 2 == 1)
    def _():
        step(ptb_ref, pta_ref)

    @pl.when(cidx == pl.num_programs(1) - 1)
    def _():
        peer = acc_ref[...].T.reshape(bb, tb, d)
        y = y_ref[...] + mod_ref[...][:, :, 5 * d:6 * d] * peer
        if final:
            y = _rms(y, fg_ref[...])
        o_ref[...] = y


def _peer(h2, y, mod, wts, final_g, layer, bb, tb, ec, final):
    bn, t, d = y.shape
    nt = t // tb
    tm = bb * tb
    n_exp = wts["peer_u"].shape[1]
    nq = wts["w_peer_q"].shape[-1]
    sub = 32
    assert tm % LANES == 0
    tile = lambda dt: pltpu.VMEM((PEER_HEADS, PEER_NKEYS, tm), dt)
    row_tile = lambda: pltpu.VMEM((PEER_HEADS, tm // LANES, PEER_NKEYS + 8, LANES), jnp.uint32)
    nc = n_exp // ec
    return pl.pallas_call(
        functools.partial(_peer_kernel, final=final, sub=sub),
        grid=(bn * t // tm, nc + 1),
        in_specs=[pl.BlockSpec((tm, d), lambda i, c: (i, 0)),
                  pl.BlockSpec((bb, tb, d), lambda i, c: (i // nt, i % nt, 0)),
                  pl.BlockSpec((bb, 1, 6 * d), lambda i, c: (i // nt, 0, 0)),
                  _const_spec((1, d, nq), lambda i, c: (layer, 0, 0)),
                  _const_spec((1, PEER_HEADS, 2, PEER_NKEYS, PEER_NKEYS), lambda i, c: (layer, 0, 0, 0, 0)),
                  pl.BlockSpec((1, ec, d), lambda i, c: (layer, jnp.minimum(c, nc - 1), 0)),
                  pl.BlockSpec((1, d, ec), lambda i, c: (layer, 0, jnp.maximum(c - 1, 0))),
                  _const_spec((1, 1, d), lambda i, c: (0, 0, 0))],
        out_specs=pl.BlockSpec((bb, tb, d), lambda i, c: (i // nt, i % nt, 0)),
        out_shape=jax.ShapeDtypeStruct((bn, t, d), F32),
        scratch_shapes=[tile(F32), tile(F32),
                        pltpu.VMEM((PEER_HEADS, PEER_TOPK, 8, tm), F32),
                        tile(BF16), tile(BF16), row_tile(), row_tile(),
                        pltpu.VMEM((ec, tm), F32),
                        pltpu.VMEM((ec, tm), F32),
                        pltpu.VMEM((ec, tm), BF16),
                        pltpu.VMEM((d, tm), F32),
                        pltpu.VMEM((d, tm), BF16)],
        compiler_params=_params(("arbitrary", "arbitrary")),
        name="peer",
    )(h2, y, mod, wts["w_peer_q"], wts["peer_keys"], wts["peer_u"], wts["peer_vt"], final_g)


def _prepare_weights(w_in, w_gla_f2, b_gla_f, gla_norm_g, b_ml_i, b_ml_f, ml_norm_g, conv_w, conv_b, w_rg_a, b_rg_a,
                     w_rg_x, b_rg_x, rg_lambda, w_br_gla, w_br_ml, w_br_rg, w_out, w_peer_q, peer_keys, peer_u, peer_v,
                     norm1_g, norm2_g):
    n_layers, d, _ = w_in.shape
    offs, o = {}, 0
    for name, w in (("gq", 512), ("gk", 512), ("gv", 1024), ("gg", 1024), ("gf", GLA_RANK),
                    ("mq", 512), ("mk", 512), ("mv", 1024), ("mo", 1024), ("mi", N_HEADS), ("mf", N_HEADS),
                    ("rx", 1024), ("mg", 3072)):
        offs[name] = (o, o + w)
        o += w
    assert o == w_in.shape[-1]
    cut = lambda name: w_in[:, :, offs[name][0]:offs[name][1]]
    w_main = jnp.concatenate([cut(n) for n, _ in IN_PIECES], axis=-1).astype(BF16)
    n_small = GLA_RANK + 2 * N_HEADS
    w_small = jnp.concatenate([cut("gf"), cut("mi"), cut("mf"),
                               jnp.zeros((n_layers, d, SMALL_W - n_small), F32)], axis=-1).astype(BF16)
    wf2 = jnp.concatenate([w_gla_f2, jnp.zeros((n_layers, SMALL_W - GLA_RANK, w_gla_f2.shape[-1]), F32)],
                          axis=1).astype(BF16)
    bias_small = jnp.concatenate([jnp.zeros((n_layers, GLA_RANK), F32), b_ml_i, b_ml_f,
                                  jnp.zeros((n_layers, SMALL_W - n_small), F32)], axis=-1)
    row = lambda a: a.reshape(n_layers, 1, a.shape[-1])
    return dict(
        w_main=w_main, w_small=w_small, wf2=wf2, b_gla_f=row(b_gla_f), gla_norm_g=row(gla_norm_g),
        bias_small=row(bias_small), ml_norm_g=row(ml_norm_g),
        conv_w=conv_w, conv_b=row(conv_b), w_rg_a=w_rg_a.astype(BF16), b_rg_a=row(b_rg_a),
        w_rg_x=w_rg_x.astype(BF16), b_rg_x=row(b_rg_x), rg_lambda=row(rg_lambda),
        w_br_gla=w_br_gla.astype(BF16), w_br_ml=w_br_ml.astype(BF16), w_br_rg=w_br_rg.astype(BF16),
        w_out=w_out.astype(BF16), w_peer_q=w_peer_q.astype(BF16), peer_keys=peer_keys.astype(BF16),
        peer_u=peer_u.astype(BF16), peer_vt=jnp.swapaxes(peer_v, 1, 2).astype(BF16),
        norm1_g=row(norm1_g), norm2_g=row(norm2_g))


def _layer(x, mod, state, wts, final_g, layer, final, cfg):
    bn, t, d = x.shape
    fresh = state is None
    act_dtype = BF16 if fresh else F32
    outs = _in_projection(x, mod, wts["norm1_g"], wts["w_main"], wts["w_small"], layer,
                          cfg["bb"], cfg["tb"], act_dtype, rx_time_major=fresh)
    p = {name: o for (name, _), o in zip(IN_PIECES, outs[:-1])}
    small = outs[-1]
    if fresh:
        s_gla = ml_state = None
        rx = p["rx"].reshape(t * bn, d)
        conv_buf = h0 = None
    else:
        s_gla, c0, n0, m0, h0, conv_buf = state
        ml_state = (c0, n0, m0)
        rx = jnp.swapaxes(p["rx"].reshape(bn, t, d), 0, 1).reshape(t * bn, d)
    o_gla, s_gla_new = _gla_mixer(p, small, wts["wf2"], wts["b_gla_f"], wts["gla_norm_g"], s_gla, layer,
                                  bn, t, cfg["gla_bb"], act_dtype)
    o_ml, c_new, n_new, m_new = _mlstm_mixer(p, small, wts["bias_small"], wts["ml_norm_g"], ml_state, layer,
                                             bn, t, cfg["ml_bb"], act_dtype)
    hs, h_fin, buf_new = _rglru(rx, conv_buf, h0, wts, layer, bn, t, cfg["rg_tc"], act_dtype)
    if fresh:
        o_rg = hs.reshape(t, bn * d)
    else:
        o_rg = jnp.swapaxes(hs.reshape(t, bn, d), 0, 1).reshape(bn * t, d)
    y, h2 = _merge(o_gla, o_ml, o_rg, p["mg"], x, mod, wts["norm2_g"], wts, layer, cfg["bb"], cfg["tb"],
                   rg_time_major=fresh)
    out = _peer(h2, y, mod, wts, final_g, layer, cfg["peer_bb"], cfg["peer_tb"], cfg["peer_ec"], final)
    new_state = (s_gla_new, c_new, n_new, m_new[:, 0, SM_I:SM_I + N_HEADS], h_fin,
                 jnp.swapaxes(buf_new.reshape(CONV_W - 1, bn, d), 0, 1))
    return out, new_state


def kernel(x_prompt, x_sample, state_gla, state_mlstm_c, state_mlstm_n, state_mlstm_m, state_rglru_h, state_rglru_conv, c_prompt, c_sample, w_ada, b_ada, norm1_g, norm2_g, w_in, w_gla_f2, b_gla_f, gla_norm_g, b_ml_i, b_ml_f, ml_norm_g, conv_w, conv_b, w_rg_a, b_rg_a, w_rg_x, b_rg_x, rg_lambda, w_br_gla, w_br_ml, w_br_rg, w_out, w_peer_q, peer_keys, peer_u, peer_v, final_norm_g):
    n_layers = w_in.shape[0]
    bp, tp, d = x_prompt.shape
    bs, ts, _ = x_sample.shape
    wts = _prepare_weights(w_in, w_gla_f2, b_gla_f, gla_norm_g, b_ml_i, b_ml_f, ml_norm_g, conv_w, conv_b, w_rg_a,
                           b_rg_a, w_rg_x, b_rg_x, rg_lambda, w_br_gla, w_br_ml, w_br_rg, w_out, w_peer_q,
                           peer_keys, peer_u, peer_v, norm1_g, norm2_g)
    final_g = final_norm_g.reshape(1, 1, d)
    mod_all = _modulation(jnp.concatenate([c_prompt, c_sample], axis=0), w_ada, b_ada)
    m0_pad = jnp.pad(state_mlstm_m, ((0, 0), (0, 0), (SM_I, SMALL_W - SM_I - N_HEADS))).reshape(n_layers, bs, 1, SMALL_W)
    conv_tm = jnp.swapaxes(state_rglru_conv, 1, 2).reshape(n_layers, (CONV_W - 1) * bs, d)
    sample_state = (state_gla, state_mlstm_c, state_mlstm_n, m0_pad, state_rglru_h, conv_tm)

    cfg_p = dict(bb=1, tb=min(256, tp), gla_bb=min(4, bp), ml_bb=min(2, bp), rg_tc=min(128, tp), peer_bb=1,
                 peer_tb=min(512, tp), peer_ec=1024)
    cfg_s = dict(bb=min(16, bs), tb=ts, gla_bb=min(4, bs), ml_bb=min(4, bs), rg_tc=ts, peer_bb=min(512 // ts, bs),
                 peer_tb=ts, peer_ec=1024)
    yp, ys = x_prompt, x_sample
    new_p, new_s = [], []
    for l in range(n_layers):
        final = l == n_layers - 1
        mod_p = mod_all[l, :bp].reshape(bp, 1, 6 * d)
        mod_s = mod_all[l, bp:].reshape(bs, 1, 6 * d)
        yp, st_p = _layer(yp, mod_p, None, wts, final_g, l, final, cfg_p)
        ys, st_s = _layer(ys, mod_s, sample_state, wts, final_g, l, final, cfg_s)
        new_p.append(st_p)
        new_s.append(st_s)
    stacked_p = [jnp.stack(z) for z in zip(*new_p)]
    stacked_s = [jnp.stack(z) for z in zip(*new_s)]
    return (yp, ys, *stacked_p, *stacked_s)
```

```python
import functools

import jax
import jax.numpy as jnp
from jax import lax
from jax.experimental import pallas as pl
from jax.experimental.pallas import tpu as pltpu

F32 = jnp.float32
BF16 = jnp.bfloat16
HIGHEST = lax.Precision.HIGHEST

EPS = 1e-6
N_HEADS = 4
DK = 128
DV = 256
GLA_RANK = 16
GLA_GATE_TEMP = 16.0
CHUNK = 64
RG_BLOCKS = 8
RG_C = 8.0
CONV_W = 4
PEER_HEADS = 8
PEER_NKEYS = 128
PEER_TOPK = 16
NEG = -1e30
LANES = 128

VMEM_LIMIT_BYTES = 56 * 1024 * 1024

IN_PIECES = (("gq", 512), ("gk", 512), ("gv", 1024), ("gg", 1024),
             ("mq", 512), ("mk", 512), ("mv", 1024), ("mo", 1024),
             ("rx", 1024), ("mg", 3072))
SMALL_W = 128
SM_I = GLA_RANK
SM_F = GLA_RANK + N_HEADS


def _params(sem):
    return pltpu.CompilerParams(dimension_semantics=sem, vmem_limit_bytes=VMEM_LIMIT_BYTES)


def _const_spec(shape, index_map):
    return pl.BlockSpec(shape, index_map, pipeline_mode=pl.Buffered(1))


def _log_sigmoid(x):
    return jnp.minimum(x, 0.0) - jnp.log1p(jnp.exp(-jnp.abs(x)))


def _softplus(x):
    return jnp.maximum(x, 0.0) + jnp.log1p(jnp.exp(-jnp.abs(x)))


def _sigmoid(x):
    return 1.0 / (1.0 + jnp.exp(-x))


def _rms(x, g):
    return x * lax.rsqrt(jnp.mean(x * x, axis=-1, keepdims=True) + EPS) * g


def _pad_rows(a, rows, value=0.0):
    if a.shape[0] == rows:
        return a
    return jnp.concatenate([a, jnp.full((rows - a.shape[0], a.shape[1]), value, a.dtype)], axis=0)


def _row_to_col(row, eye):
    n = row.shape[1]
    return jnp.sum(jnp.where(eye, jnp.broadcast_to(row, (n, n)), 0.0), axis=1, keepdims=True)


def _col_to_row(col, eye):
    n = col.shape[0]
    return jnp.sum(jnp.where(eye, jnp.broadcast_to(col, (n, n)), 0.0), axis=0, keepdims=True)


def _mod_kernel(c_ref, w_ref, b_ref, o_ref):
    c = c_ref[...]
    s = (c * _sigmoid(c)).astype(BF16)
    o_ref[0] = jnp.dot(s, w_ref[0].astype(BF16), preferred_element_type=F32) + b_ref[0]


def _modulation(c_all, w_ada, b_ada):
    n_layers, d, n6 = w_ada.shape
    rows = c_all.shape[0]
    tn = 1536
    return pl.pallas_call(
        _mod_kernel,
        grid=(n_layers, n6 // tn),
        in_specs=[pl.BlockSpec((rows, d), lambda l, n: (0, 0)),
                  pl.BlockSpec((1, d, tn), lambda l, n: (l, 0, n)),
                  pl.BlockSpec((1, 1, tn), lambda l, n: (l, 0, n))],
        out_specs=pl.BlockSpec((1, rows, tn), lambda l, n: (l, 0, n)),
        out_shape=jax.ShapeDtypeStruct((n_layers, rows, n6), F32),
        compiler_params=_params(("arbitrary", "arbitrary")),
        name="adaln_mod",
    )(c_all, w_ada, b_ada.reshape(n_layers, 1, n6))


def _inproj_kernel(x_ref, mod_ref, g_ref, wm_ref, ws_ref, *out_refs):
    bb, tb, d = x_ref.shape
    mod = mod_ref[...]
    h = _rms(x_ref[...], g_ref[...]) * (1.0 + mod[:, :, d:2 * d]) + mod[:, :, 0:d]
    hb = h.reshape(bb * tb, d).astype(BF16)
    off = 0
    for r, (_, w) in zip(out_refs[:-1], IN_PIECES):
        r[...] = jnp.dot(hb, wm_ref[0, :, off:off + w], preferred_element_type=F32).astype(r.dtype)
        off += w
    out_refs[-1][...] = jnp.dot(hb, ws_ref[0], preferred_element_type=F32)


def _in_projection(x, mod, g1, w_main, w_small, layer, bb, tb, out_dtype, rx_time_major):
    bn, t, d = x.shape
    nt = t // tb
    rows = bb * tb
    n_main = w_main.shape[-1]
    out_shapes, out_specs = [], []
    for name, w in IN_PIECES:
        if name == "rx" and rx_time_major:
            out_shapes.append(jax.ShapeDtypeStruct((t, bn * w), out_dtype))
            out_specs.append(pl.BlockSpec((tb, w), lambda i, j: (j, i)))
        else:
            out_shapes.append(jax.ShapeDtypeStruct((bn * t, w), out_dtype))
            out_specs.append(pl.BlockSpec((rows, w), lambda i, j: (i * nt + j, 0)))
    out_shapes.append(jax.ShapeDtypeStruct((bn * t, SMALL_W), F32))
    out_specs.append(pl.BlockSpec((rows, SMALL_W), lambda i, j: (i * nt + j, 0)))
    if rx_time_major:
        assert bb == 1
    return pl.pallas_call(
        _inproj_kernel,
        grid=(bn // bb, nt),
        in_specs=[pl.BlockSpec((bb, tb, d), lambda i, j: (i, j, 0)),
                  pl.BlockSpec((bb, 1, 6 * d), lambda i, j: (i, 0, 0)),
                  _const_spec((1, 1, d), lambda i, j: (layer, 0, 0)),
                  _const_spec((1, d, n_main), lambda i, j: (layer, 0, 0)),
                  _const_spec((1, d, SMALL_W), lambda i, j: (layer, 0, 0))],
        out_specs=out_specs,
        out_shape=out_shapes,
        compiler_params=_params(("arbitrary", "arbitrary")),
        name="in_proj",
    )(x, mod, g1, w_main, w_small)


def _gla_kernel(*refs, bb, tc, has_state, has_stack):
    if has_stack:
        refs = refs[:-3] + refs[-2:]
    if has_state:
        gq, gk, gv, gg, sm, wf2, bf, gn, s0, o_ref, s_ref = refs
    else:
        gq, gk, gv, gg, sm, wf2, bf, gn, o_ref, s_ref = refs
        s0 = None
    c = CHUNK

    @pl.when(pl.program_id(1) == 0)
    def _():
        if has_state:
            s_ref[...] = s0[...]
        else:
            s_ref[...] = jnp.zeros_like(s_ref)

    ri = lax.broadcasted_iota(jnp.int32, (c, c), 0)
    ci = lax.broadcasted_iota(jnp.int32, (c, c), 1)
    causal = ri >= ci
    tril = causal.astype(F32)
    hk = N_HEADS * DK
    valid = lax.broadcasted_iota(jnp.int32, (c, hk), 0) < tc
    eye = (lax.broadcasted_iota(jnp.int32, (DK, DK), 0) == lax.broadcasted_iota(jnp.int32, (DK, DK), 1))
    nt_dims = (((1,), (1,)), ((), ()))
    tn_dims = (((0,), (0,)), ((), ()))
    for s in range(bb):
        smb = _pad_rows(sm[s], c).astype(BF16)
        x = jnp.dot(smb, wf2[0], preferred_element_type=F32) + bf[0]
        logf = jnp.where(valid, _log_sigmoid(x) * (1.0 / GLA_GATE_TEMP), 0.0)
        b = jnp.dot(tril, logf, precision=HIGHEST, preferred_element_type=F32)
        b_last = b[c - 1:c, :]
        q = _pad_rows(gq[s].astype(F32), c) * (DK ** -0.5)
        k = _pad_rows(gk[s].astype(F32), c)
        qd = (q * jnp.exp(b)).astype(BF16)
        kd = (k * jnp.exp(-b)).astype(BF16)
        kend = (k * jnp.exp(b_last - b)).astype(BF16)
        vb = _pad_rows(gv[s].astype(F32), c).astype(BF16)
        g = gg[s].astype(F32)
        g = g * _sigmoid(g)
        for h in range(N_HEADS):
            kcols = slice(h * DK, (h + 1) * DK)
            vcols = slice(h * DV, (h + 1) * DV)
            att = lax.dot_general(qd[:, kcols], kd[:, kcols], nt_dims, preferred_element_type=F32)
            att = jnp.where(causal, att, 0.0).astype(BF16)
            st = s_ref[s, h]
            o = (jnp.dot(att, vb[:, vcols], preferred_element_type=F32)
                 + jnp.dot(qd[:, kcols], st.astype(BF16), preferred_element_type=F32))
            dec = jnp.exp(_row_to_col(b_last[:, kcols], eye))
            s_ref[s, h] = dec * st + lax.dot_general(kend[:, kcols], vb[:, vcols], tn_dims,
                                                     preferred_element_type=F32)
            o_ref[s, :, vcols] = (_rms(o[:tc], gn[0]) * g[:, vcols]).astype(o_ref.dtype)


def _gla_mixer(p, small, wf2, bfv, gn, s0, layer, n_layers, stack, bn, t, bb, out_dtype):
    tc = min(CHUNK, t)
    nch = t // tc
    assert tc * nch == t
    rspec = lambda w: pl.BlockSpec((bb, tc, w), lambda i, j: (i, j, 0))
    r3 = lambda a: a.reshape(bn, t, a.shape[-1])
    in_specs = [rspec(512), rspec(512), rspec(1024), rspec(1024), rspec(SMALL_W),
                _const_spec((1, SMALL_W, 512), lambda i, j: (layer, 0, 0)),
                _const_spec((1, 1, 512), lambda i, j: (layer, 0, 0)),
                _const_spec((1, 1, DV), lambda i, j: (layer, 0, 0))]
    args = [r3(p["gq"]), r3(p["gk"]), r3(p["gv"]), r3(p["gg"]), r3(small), wf2, bfv, gn]
    sspec = pl.BlockSpec((None, bb, N_HEADS, DK, DV), lambda i, j: (layer, i, 0, 0, 0))
    if s0 is not None:
        in_specs.append(pl.BlockSpec((None, bb, N_HEADS, DK, DV), lambda i, j: (layer, i, 0, 0, 0)))
        args.append(s0)
    aliases = {}
    if stack is not None:
        in_specs.append(pl.BlockSpec(memory_space=pl.ANY))
        args.append(stack)
        aliases = {len(args) - 1: 1}
    o, s_stack = pl.pallas_call(
        functools.partial(_gla_kernel, bb=bb, tc=tc, has_state=s0 is not None, has_stack=stack is not None),
        grid=(bn // bb, nch),
        in_specs=in_specs,
        out_specs=[rspec(1024), sspec],
        out_shape=[jax.ShapeDtypeStruct((bn, t, 1024), out_dtype),
                   jax.ShapeDtypeStruct((n_layers, bn, N_HEADS, DK, DV), F32)],
        input_output_aliases=aliases,
        compiler_params=_params(("arbitrary", "arbitrary")),
        name="gla_mixer",
    )(*args)
    return o.reshape(bn * t, 1024), s_stack


def _mlstm_kernel(*refs, bb, tc, has_state, has_stack):
    if has_stack:
        refs = refs[:-5] + refs[-4:]
    if has_state:
        mq, mk, mv, mo, sm, bias, gn, c0, n0, m0, o_ref, c_ref, n_ref, m_ref = refs
    else:
        mq, mk, mv, mo, sm, bias, gn, o_ref, c_ref, n_ref, m_ref = refs
    c = CHUNK

    @pl.when(pl.program_id(1) == 0)
    def _():
        if has_state:
            c_ref[...] = c0[...]
            n_ref[...] = n0[...]
            m_ref[...] = m0[...]
        else:
            c_ref[...] = jnp.zeros_like(c_ref)
            n_ref[...] = jnp.zeros_like(n_ref)
            m_ref[...] = jnp.zeros_like(m_ref)

    ri = lax.broadcasted_iota(jnp.int32, (c, c), 0)
    ci = lax.broadcasted_iota(jnp.int32, (c, c), 1)
    causal = ri >= ci
    eye = ri == ci
    tril = causal.astype(F32)
    trow = lax.broadcasted_iota(jnp.int32, (c, SMALL_W), 0)
    valid = trow < tc
    nt_dims = (((1,), (1,)), ((), ()))
    tn_dims = (((0,), (0,)), ((), ()))
    for s in range(bb):
        t1 = _pad_rows(sm[s], c) + bias[0]
        ig_t = jnp.where(valid, t1, NEG)
        lf_t = jnp.where(valid, _log_sigmoid(t1), 0.0)
        f_t = jnp.dot(tril, lf_t, precision=HIGHEST, preferred_element_type=F32)
        fa_t = pltpu.roll(f_t, SMALL_W - (SM_F - SM_I), 1)
        a_t = ig_t - fa_t
        cm_t = a_t
        for sh in (1, 2, 4, 8, 16, 32):
            cm_t = jnp.maximum(cm_t, jnp.where(trow >= sh, pltpu.roll(cm_t, sh, 0), NEG))
        m_prev = m_ref[s]
        m_t = fa_t + jnp.maximum(m_prev, cm_t)
        wi_t = jnp.exp(fa_t + m_prev - m_t)
        fm_t = fa_t - m_t
        em_t = jnp.exp(-m_t)
        m_last_row = m_t[c - 1:c, :]
        f_last_row = fa_t[c - 1:c, :]
        ws_t = jnp.exp(ig_t + f_last_row - fa_t - m_last_row)
        carry_row = jnp.exp(f_last_row + m_prev - m_last_row)
        qb_all = _pad_rows(mq[s].astype(F32), c).astype(BF16)
        k_all = _pad_rows(mk[s].astype(F32), c) * (DK ** -0.5)
        vb_all = _pad_rows(mv[s].astype(F32), c).astype(BF16)
        g_all = _sigmoid(mo[s].astype(F32))
        for h in range(N_HEADS):
            kcols = slice(h * DK, (h + 1) * DK)
            vcols = slice(h * DV, (h + 1) * DV)
            qb = qb_all[:, kcols]
            q = qb.astype(F32)
            k = k_all[:, kcols]
            vb = vb_all[:, vcols]
            col = slice(SM_I + h, SM_I + h + 1)
            a_row = _col_to_row(a_t[:, col], eye)
            w_inter = wi_t[:, col]
            dmat = jnp.exp(jnp.where(causal, fm_t[:, col] + a_row, NEG))
            qk = lax.dot_general(qb, k.astype(BF16), nt_dims, preferred_element_type=F32) * dmat
            cs = c_ref[s, h]
            ns = n_ref[s, h:h + 1, :]
            num = (jnp.dot(qk.astype(BF16), vb, preferred_element_type=F32)
                   + w_inter * jnp.dot(qb, cs.astype(BF16), preferred_element_type=F32))
            den = jnp.sum(qk, axis=1, keepdims=True) + w_inter * jnp.sum(q * ns, axis=1, keepdims=True)
            hh = num / jnp.maximum(jnp.abs(den), em_t[:, col])
            carry = carry_row[:, col]
            kw = k * ws_t[:, col]
            c_ref[s, h] = carry * cs + lax.dot_general(kw.astype(BF16), vb, tn_dims, preferred_element_type=F32)
            n_ref[s, h:h + 1, :] = carry * ns + jnp.sum(kw, axis=0, keepdims=True)
            o_ref[s, :, vcols] = (_rms(hh[:tc], gn[0]) * g_all[:, vcols]).astype(o_ref.dtype)
        m_ref[s] = m_last_row


def _mlstm_mixer(p, small, bias, gn, state, layer, n_layers, stack, bn, t, bb, out_dtype):
    tc = min(CHUNK, t)
    nch = t // tc
    assert tc * nch == t
    rspec = lambda w: pl.BlockSpec((bb, tc, w), lambda i, j: (i, j, 0))
    r3 = lambda a: a.reshape(bn, t, a.shape[-1])
    in_specs = [rspec(512), rspec(512), rspec(1024), rspec(1024), rspec(SMALL_W),
                _const_spec((1, 1, SMALL_W), lambda i, j: (layer, 0, 0)),
                _const_spec((1, 1, DV), lambda i, j: (layer, 0, 0))]
    args = [r3(p["mq"]), r3(p["mk"]), r3(p["mv"]), r3(p["mo"]), r3(small), bias, gn]
    cspec = pl.BlockSpec((None, bb, N_HEADS, DK, DV), lambda i, j: (layer, i, 0, 0, 0))
    nspec = pl.BlockSpec((bb, N_HEADS, DK), lambda i, j: (i, 0, 0))
    mspec = pl.BlockSpec((bb, 1, SMALL_W), lambda i, j: (i, 0, 0))
    if state is not None:
        c0, n0, m0 = state
        in_specs += [pl.BlockSpec((None, bb, N_HEADS, DK, DV), lambda i, j: (layer, i, 0, 0, 0)),
                     pl.BlockSpec((None, bb, N_HEADS, DK), lambda i, j: (layer, i, 0, 0)),
                     pl.BlockSpec((None, bb, 1, SMALL_W), lambda i, j: (layer, i, 0, 0))]
        args += [c0, n0, m0]
    aliases = {}
    if stack is not None:
        in_specs.append(pl.BlockSpec(memory_space=pl.ANY))
        args.append(stack)
        aliases = {len(args) - 1: 1}
    o, c_stack, n_new, m_new = pl.pallas_call(
        functools.partial(_mlstm_kernel, bb=bb, tc=tc, has_state=state is not None, has_stack=stack is not None),
        grid=(bn // bb, nch),
        in_specs=in_specs,
        out_specs=[rspec(1024), cspec, nspec, mspec],
        out_shape=[jax.ShapeDtypeStruct((bn, t, 1024), out_dtype),
                   jax.ShapeDtypeStruct((n_layers, bn, N_HEADS, DK, DV), F32),
                   jax.ShapeDtypeStruct((bn, N_HEADS, DK), F32),
                   jax.ShapeDtypeStruct((bn, 1, SMALL_W), F32)],
        input_output_aliases=aliases,
        compiler_params=_params(("arbitrary", "arbitrary")),
        name="mlstm_mixer",
    )(*args)
    return o.reshape(bn * t, 1024), c_stack, n_new, m_new


def _rglru_kernel(*refs, nb, tc, has_state):
    if has_state:
        rx, cb, h0, cw, cbias, wa, ba, wx, bx, lam, hs_ref, hf_ref, nb_ref, xpad, a_s, u_s = refs
    else:
        rx, cw, cbias, wa, ba, wx, bx, lam, hs_ref, hf_ref, nb_ref, xpad, a_s, u_s = refs
    hist = (CONV_W - 1) * nb
    rows = tc * nb
    w = rx.shape[1]
    bw = w // RG_BLOCKS

    @pl.when(pl.program_id(0) == 0)
    def _():
        if has_state:
            xpad[0:hist, :] = cb[0]
            hf_ref[...] = h0[0]
        else:
            xpad[0:hist, :] = jnp.zeros((hist, w), F32)
            hf_ref[...] = jnp.zeros_like(hf_ref)

    xpad[hist:hist + rows, :] = rx[...].astype(F32)
    xc = cbias[0]
    for j in range(CONV_W):
        xc = xc + xpad[j * nb:j * nb + rows, :] * cw[0, j:j + 1, :]
    sp = _softplus(-lam[0])
    for n in range(RG_BLOCKS):
        cols = slice(n * bw, (n + 1) * bw)
        xn = xc[:, cols]
        xg = xn.astype(BF16)
        r = _sigmoid(jnp.dot(xg, wa[0, n], preferred_element_type=F32) + ba[0, :, cols])
        i = _sigmoid(jnp.dot(xg, wx[0, n], preferred_element_type=F32) + bx[0, :, cols])
        log_a = -RG_C * r * sp[:, cols]
        a_s[:, cols] = jnp.exp(log_a)
        u_s[:, cols] = jnp.sqrt(1.0 - jnp.exp(2.0 * log_a)) * (i * xn)

    def body(t, h):
        o = pl.multiple_of(t * nb, nb)
        h = a_s[pl.ds(o, nb), :] * h + u_s[pl.ds(o, nb), :]
        hs_ref[pl.ds(o, nb), :] = h.astype(hs_ref.dtype)
        return h

    hf_ref[...] = lax.fori_loop(0, tc, body, hf_ref[...])
    tail = xpad[rows:rows + hist, :]
    xpad[0:hist, :] = tail
    nb_ref[...] = tail


def _rglru(rx, conv_buf, h0, wts, layer, nb, t, tc, out_dtype):
    w = rx.shape[1]
    rows = tc * nb
    hist = (CONV_W - 1) * nb
    bw = w // RG_BLOCKS
    lsp = lambda shape: _const_spec((1,) + shape, lambda j: (layer,) + (0,) * len(shape))
    in_specs = [pl.BlockSpec((rows, w), lambda j: (j, 0))]
    args = [rx]
    if conv_buf is not None:
        in_specs += [lsp((hist, w)), lsp((nb, w))]
        args += [conv_buf, h0]
    in_specs += [lsp((CONV_W, w)), lsp((1, w)), lsp((RG_BLOCKS, bw, bw)), lsp((1, w)),
                 lsp((RG_BLOCKS, bw, bw)), lsp((1, w)), lsp((1, w))]
    args += [wts["conv_w"], wts["conv_b"], wts["w_rg_a"], wts["b_rg_a"], wts["w_rg_x"], wts["b_rg_x"],
             wts["rg_lambda"]]
    return pl.pallas_call(
        functools.partial(_rglru_kernel, nb=nb, tc=tc, has_state=conv_buf is not None),
        grid=(t // tc,),
        in_specs=in_specs,
        out_specs=[pl.BlockSpec((rows, w), lambda j: (j, 0)),
                   pl.BlockSpec((nb, w), lambda j: (0, 0)),
                   pl.BlockSpec((hist, w), lambda j: (0, 0))],
        out_shape=[jax.ShapeDtypeStruct((t * nb, w), out_dtype),
                   jax.ShapeDtypeStruct((nb, w), F32),
                   jax.ShapeDtypeStruct((hist, w), F32)],
        scratch_shapes=[pltpu.VMEM((rows + hist, w), F32), pltpu.VMEM((rows, w), F32), pltpu.VMEM((rows, w), F32)],
        compiler_params=_params(("arbitrary",)),
        name="rglru",
    )(*args)


def _merge_kernel(og, om, orr, mg, x_ref, mod_ref, g2, wbg, wbm, wbr, wo, y_ref, h2_ref):
    bb, tb, d = x_ref.shape
    gates = mg[...].astype(F32)
    merged = (_sigmoid(gates[:, 0:d]) * jnp.dot(og[...].astype(BF16), wbg[0], preferred_element_type=F32)
              + _sigmoid(gates[:, d:2 * d]) * jnp.dot(om[...].astype(BF16), wbm[0], preferred_element_type=F32)
              + _sigmoid(gates[:, 2 * d:3 * d]) * jnp.dot(orr[...].astype(BF16), wbr[0], preferred_element_type=F32))
    z = jnp.dot(merged.astype(BF16), wo[0], preferred_element_type=F32)
    mod = mod_ref[...]
    y = x_ref[...] + mod[:, :, 2 * d:3 * d] * z.reshape(bb, tb, d)
    y_ref[...] = y
    h2 = _rms(y, g2[...]) * (1.0 + mod[:, :, 4 * d:5 * d]) + mod[:, :, 3 * d:4 * d]
    h2_ref[...] = h2.reshape(bb * tb, d).astype(BF16)


def _merge(og, om, orr, mg, x, mod, g2, wts, layer, bb, tb, rg_time_major):
    bn, t, d = x.shape
    nt = t // tb
    rows = bb * tb
    rspec = lambda w: pl.BlockSpec((rows, w), lambda i, j: (i * nt + j, 0))
    if rg_time_major:
        assert bb == 1
        rg_spec = pl.BlockSpec((tb, d), lambda i, j: (j, i))
    else:
        rg_spec = rspec(d)
    wspec = _const_spec((1, d, d), lambda i, j: (layer, 0, 0))
    return pl.pallas_call(
        _merge_kernel,
        grid=(bn // bb, nt),
        in_specs=[rspec(d), rspec(d), rg_spec, rspec(3 * d),
                  pl.BlockSpec((bb, tb, d), lambda i, j: (i, j, 0)),
                  pl.BlockSpec((bb, 1, 6 * d), lambda i, j: (i, 0, 0)),
                  _const_spec((1, 1, d), lambda i, j: (layer, 0, 0)),
                  wspec, wspec, wspec, wspec],
        out_specs=[pl.BlockSpec((bb, tb, d), lambda i, j: (i, j, 0)), rspec(d)],
        out_shape=[jax.ShapeDtypeStruct((bn, t, d), F32), jax.ShapeDtypeStruct((bn * t, d), BF16)],
        compiler_params=_params(("arbitrary", "arbitrary")),
        name="merge",
    )(og, om, orr, mg, x, mod, g2, wts["w_br_gla"], wts["w_br_ml"], wts["w_br_rg"], wts["w_out"])


def _sort16_pairs():
    n, pairs, p = 16, [], 1
    while p < n:
        k = p
        while k >= 1:
            for j in range(k % p, n - k, 2 * k):
                for i in range(min(k, n - j - k)):
                    if (i + j) // (2 * p) == (i + j + k) // (2 * p):
                        pairs.append((i + j, i + j + k))
            k //= 2
        p *= 2
    return pairs


_SORT16 = _sort16_pairs()
_BITONIC16 = [(i, i + dd) for dd in (8, 4, 2, 1) for i in range(16) if (i & dd) == 0]
_CANDS = [(a, b) for a in range(PEER_TOPK) for b in range(PEER_TOPK) if (a + 1) * (b + 1) <= PEER_TOPK]


def _top16_sorted(st):
    xs = [st[g * 8:(g + 1) * 8, :] for g in range(16)]
    for i, j in _SORT16:
        hi, lo = jnp.maximum(xs[i], xs[j]), jnp.minimum(xs[i], xs[j])
        xs[i], xs[j] = hi, lo
    for shift in (4, 2, 1):
        ys = [jnp.maximum(xs[r], pltpu.roll(xs[15 - r], shift, 0)) for r in range(16)]
        for i, j in _BITONIC16:
            hi, lo = jnp.maximum(ys[i], ys[j]), jnp.minimum(ys[i], ys[j])
            ys[i], ys[j] = hi, lo
        xs = ys
    return xs


def _dup_bf16_words(x):
    bits = lax.bitcast_convert_type(x.astype(BF16).astype(F32), jnp.uint32)
    return bits | (bits >> 16)


def _peer_kernel(h2_ref, y_ref, mod_ref, wq_ref, keys_ref, u_ref, vt_ref, fg_ref, o_ref,
                 s0_ref, s1_ref, sv1_ref, r1b_ref, e1b_ref, n0w_ref, ccw_ref, pta_ref, ptb_ref, gt_ref, acc_ref, h2t_ref,
                 *, final, sub):
    bb, tb, d = y_ref.shape
    tm = bb * tb
    ec = u_ref.shape[1]
    cidx = pl.program_id(1)
    nt_dims = (((1,), (1,)), ((), ()))
    ngrp = PEER_NKEYS // 8

    @pl.when(cidx == 0)
    def _():
        q = jnp.dot(h2_ref[...], wq_ref[0], preferred_element_type=F32)
        for p, s_ref in enumerate((s0_ref, s1_ref)):
            for h in range(PEER_HEADS):
                col = (h * 2 + p) * PEER_NKEYS
                qs = q[:, col:col + PEER_NKEYS].astype(BF16)
                s_ref[h] = lax.dot_general(keys_ref[0, h, p], qs, nt_dims, preferred_element_type=F32)
        sub8 = lax.broadcasted_iota(jnp.int32, (8, LANES), 0)
        rep = lambda row: jnp.broadcast_to(row, (8, LANES))[None]
        for l in range(tm // LANES):
            lanes = slice(l * LANES, (l + 1) * LANES)
            tops = []
            for p, s_ref in enumerate((s0_ref, s1_ref)):
                stack = [jnp.zeros((8, LANES), F32) for _ in range(PEER_TOPK)]
                for h in range(PEER_HEADS):
                    st = s_ref[h, :, lanes]
                    srt = _top16_sorted(st)
                    if p == 1:
                        st3 = st.reshape(ngrp, 8, LANES)
                        rank = jnp.full((ngrp, 8, LANES), float(PEER_TOPK), F32)
                        for r in reversed(range(PEER_TOPK)):
                            sv1_ref[h, r, :, lanes] = srt[r]
                            rank = jnp.where(srt[r][None] > st3, rank, float(r))
                        r1b_ref[h, :, lanes] = rank.reshape(PEER_NKEYS, LANES).astype(BF16)
                        e1b_ref[h, :, lanes] = jnp.exp(st3 - srt[0][None]).reshape(PEER_NKEYS, LANES).astype(BF16)
                    stack = [jnp.where(sub8 == h, srt[r], stack[r]) for r in range(PEER_TOPK)]
                tops.append(stack)
            work = [tops[0][a] + tops[1][b] for a, b in _CANDS]
            theta = jnp.full((8, LANES), NEG, F32)
            cnt = jnp.zeros((8, LANES), F32)
            for _ in range(PEER_TOPK):
                mx = functools.reduce(jnp.maximum, work)
                eq = [w == mx for w in work]
                theta = jnp.where(cnt < PEER_TOPK, mx, theta)
                cnt = cnt + functools.reduce(jnp.add, [e.astype(F32) for e in eq])
                work = [jnp.where(e, NEG, w) for e, w in zip(eq, work)]
            top = tops[0][0] + tops[1][0]
            z = jnp.zeros((8, LANES), F32)
            for a, b in _CANDS:
                cd = tops[0][a] + tops[1][b]
                z = z + jnp.where(cd >= theta, jnp.exp(cd - top), 0.0)
            inv_z = 1.0 / z
            for h in range(PEER_HEADS):
                s03 = s0_ref[h, :, lanes].reshape(ngrp, 8, LANES)
                th8 = rep(theta[h:h + 1, :])
                n0 = jnp.full((ngrp, 8, LANES), float(PEER_TOPK), F32)
                for b in reversed(range(PEER_TOPK)):
                    n0 = jnp.where(s03 + sv1_ref[h, b, :, lanes][None] >= th8, n0, float(b))
                cc = jnp.exp(s03 - rep(tops[0][0][h:h + 1, :])) * rep(inv_z[h:h + 1, :])
                n0w_ref[h, l, 0:PEER_NKEYS, :] = _dup_bf16_words(n0.reshape(PEER_NKEYS, LANES))
                ccw_ref[h, l, 0:PEER_NKEYS, :] = _dup_bf16_words(cc.reshape(PEER_NKEYS, LANES))
        acc_ref[...] = jnp.zeros_like(acc_ref)
        ptb_ref[...] = jnp.zeros_like(ptb_ref)
        h2t_ref[...] = h2_ref[...].astype(F32).T.astype(BF16)

    groups = ec // PEER_NKEYS
    nsub = PEER_NKEYS // sub
    prev = jnp.maximum(cidx - 1, 0)

    def step(pt_next, pt_cur):
        pt_next[...] = jnp.dot(u_ref[0], h2t_ref[...], preferred_element_type=F32)

        def row(ref, h, i):
            return jnp.concatenate([pltpu.bitcast(ref[h, l, pl.ds(i, sub // 2, stride=0), :], BF16)
                                    for l in range(tm // LANES)], axis=1)

        for g in range(groups):
            i = prev * groups + g
            for jb in range(nsub):
                j0 = jb * sub
                r0 = g * PEER_NKEYS + j0
                pre = pt_cur[r0:r0 + sub, :]
                act = (0.5 * pre * (1.0 + lax.erf(pre * 0.7071067811865476))).astype(BF16)
                wgt = jnp.zeros((sub, tm), BF16)
                for h in range(PEER_HEADS):
                    val = e1b_ref[h, j0:j0 + sub, :] * row(ccw_ref, h, i)
                    wgt = wgt + jnp.where(r1b_ref[h, j0:j0 + sub, :] < row(n0w_ref, h, i), val, jnp.zeros_like(val))
                gt_ref[r0:r0 + sub, :] = act * wgt
        acc_ref[...] += jnp.dot(vt_ref[0], gt_ref[...], preferred_element_type=F32)

    @pl.when(cidx % 2 == 0)
    def _():
        step(pta_ref, ptb_ref)

    @pl.when(cidx % 2 == 1)
    def _():
        step(ptb_ref, pta_ref)

    @pl.when(cidx == pl.num_programs(1) - 1)
    def _():
        peer = acc_ref[...].T.reshape(bb, tb, d)
        y = y_ref[...] + mod_ref[...][:, :, 5 * d:6 * d] * peer
        if final:
            y = _rms(y, fg_ref[...])
        o_ref[...] = y


def _peer(h2, y, mod, wts, final_g, layer, bb, tb, ec, final):
    bn, t, d = y.shape
    nt = t // tb
    tm = bb * tb
    n_exp = wts["peer_u"].shape[1]
    nq = wts["w_peer_q"].shape[-1]
    sub = 32
    assert tm % LANES == 0
    tile = lambda dt: pltpu.VMEM((PEER_HEADS, PEER_NKEYS, tm), dt)
    row_tile = lambda: pltpu.VMEM((PEER_HEADS, tm // LANES, PEER_NKEYS + 8, LANES), jnp.uint32)
    nc = n_exp // ec
    return pl.pallas_call(
        functools.partial(_peer_kernel, final=final, sub=sub),
        grid=(bn * t // tm, nc + 1),
        in_specs=[pl.BlockSpec((tm, d), lambda i, c: (i, 0)),
                  pl.BlockSpec((bb, tb, d), lambda i, c: (i // nt, i % nt, 0)),
                  pl.BlockSpec((bb, 1, 6 * d), lambda i, c: (i // nt, 0, 0)),
                  _const_spec((1, d, nq), lambda i, c: (layer, 0, 0)),
                  _const_spec((1, PEER_HEADS, 2, PEER_NKEYS, PEER_NKEYS), lambda i, c: (layer, 0, 0, 0, 0)),
                  pl.BlockSpec((1, ec, d), lambda i, c: (layer, jnp.minimum(c, nc - 1), 0)),
                  pl.BlockSpec((1, d, ec), lambda i, c: (layer, 0, jnp.maximum(c - 1, 0))),
                  _const_spec((1, 1, d), lambda i, c: (0, 0, 0))],
        out_specs=pl.BlockSpec((bb, tb, d), lambda i, c: (i // nt, i % nt, 0)),
        out_shape=jax.ShapeDtypeStruct((bn, t, d), F32),
        scratch_shapes=[tile(F32), tile(F32),
                        pltpu.VMEM((PEER_HEADS, PEER_TOPK, 8, tm), F32),
                        tile(BF16), tile(BF16), row_tile(), row_tile(),
                        pltpu.VMEM((ec, tm), F32),
                        pltpu.VMEM((ec, tm), F32),
                        pltpu.VMEM((ec, tm), BF16),
                        pltpu.VMEM((d, tm), F32),
                        pltpu.VMEM((d, tm), BF16)],
        compiler_params=_params(("arbitrary", "arbitrary")),
        name="peer",
    )(h2, y, mod, wts["w_peer_q"], wts["peer_keys"], wts["peer_u"], wts["peer_vt"], final_g)


def _prepare_weights(w_in, w_gla_f2, b_gla_f, gla_norm_g, b_ml_i, b_ml_f, ml_norm_g, conv_w, conv_b, w_rg_a, b_rg_a,
                     w_rg_x, b_rg_x, rg_lambda, w_br_gla, w_br_ml, w_br_rg, w_out, w_peer_q, peer_keys, peer_u, peer_v,
                     norm1_g, norm2_g):
    n_layers, d, _ = w_in.shape
    offs, o = {}, 0
    for name, w in (("gq", 512), ("gk", 512), ("gv", 1024), ("gg", 1024), ("gf", GLA_RANK),
                    ("mq", 512), ("mk", 512), ("mv", 1024), ("mo", 1024), ("mi", N_HEADS), ("mf", N_HEADS),
                    ("rx", 1024), ("mg", 3072)):
        offs[name] = (o, o + w)
        o += w
    assert o == w_in.shape[-1]
    cut = lambda name: w_in[:, :, offs[name][0]:offs[name][1]]
    w_main = jnp.concatenate([cut(n) for n, _ in IN_PIECES], axis=-1).astype(BF16)
    n_small = GLA_RANK + 2 * N_HEADS
    w_small = jnp.concatenate([cut("gf"), cut("mi"), cut("mf"),
                               jnp.zeros((n_layers, d, SMALL_W - n_small), F32)], axis=-1).astype(BF16)
    wf2 = jnp.concatenate([w_gla_f2, jnp.zeros((n_layers, SMALL_W - GLA_RANK, w_gla_f2.shape[-1]), F32)],
                          axis=1).astype(BF16)
    bias_small = jnp.concatenate([jnp.zeros((n_layers, GLA_RANK), F32), b_ml_i, b_ml_f,
                                  jnp.zeros((n_layers, SMALL_W - n_small), F32)], axis=-1)
    row = lambda a: a.reshape(n_layers, 1, a.shape[-1])
    return dict(
        w_main=w_main, w_small=w_small, wf2=wf2, b_gla_f=row(b_gla_f), gla_norm_g=row(gla_norm_g),
        bias_small=row(bias_small), ml_norm_g=row(ml_norm_g),
        conv_w=conv_w, conv_b=row(conv_b), w_rg_a=w_rg_a.astype(BF16), b_rg_a=row(b_rg_a),
        w_rg_x=w_rg_x.astype(BF16), b_rg_x=row(b_rg_x), rg_lambda=row(rg_lambda),
        w_br_gla=w_br_gla.astype(BF16), w_br_ml=w_br_ml.astype(BF16), w_br_rg=w_br_rg.astype(BF16),
        w_out=w_out.astype(BF16), w_peer_q=w_peer_q.astype(BF16), peer_keys=peer_keys.astype(BF16),
        peer_u=peer_u.astype(BF16), peer_vt=jnp.swapaxes(peer_v, 1, 2).astype(BF16),
        norm1_g=row(norm1_g), norm2_g=row(norm2_g))


def _layer(x, mod, state, wts, final_g, layer, n_layers, stacks, cfg):
    bn, t, d = x.shape
    final = layer == n_layers - 1
    gla_stack, mlc_stack = stacks if stacks is not None else (None, None)
    fresh = state is None
    act_dtype = BF16 if fresh else F32
    outs = _in_projection(x, mod, wts["norm1_g"], wts["w_main"], wts["w_small"], layer,
                          cfg["bb"], cfg["tb"], act_dtype, rx_time_major=fresh)
    p = {name: o for (name, _), o in zip(IN_PIECES, outs[:-1])}
    small = outs[-1]
    if fresh:
        s_gla = ml_state = None
        rx = p["rx"].reshape(t * bn, d)
        conv_buf = h0 = None
    else:
        s_gla, c0, n0, m0, h0, conv_buf = state
        ml_state = (c0, n0, m0)
        rx = jnp.swapaxes(p["rx"].reshape(bn, t, d), 0, 1).reshape(t * bn, d)
    o_gla, gla_stack = _gla_mixer(p, small, wts["wf2"], wts["b_gla_f"], wts["gla_norm_g"], s_gla, layer,
                                  n_layers, gla_stack, bn, t, cfg["gla_bb"], act_dtype)
    o_ml, mlc_stack, n_new, m_new = _mlstm_mixer(p, small, wts["bias_small"], wts["ml_norm_g"], ml_state, layer,
                                                 n_layers, mlc_stack, bn, t, cfg["ml_bb"], act_dtype)
    hs, h_fin, buf_new = _rglru(rx, conv_buf, h0, wts, layer, bn, t, cfg["rg_tc"], act_dtype)
    if fresh:
        o_rg = hs.reshape(t, bn * d)
    else:
        o_rg = jnp.swapaxes(hs.reshape(t, bn, d), 0, 1).reshape(bn * t, d)
    y, h2 = _merge(o_gla, o_ml, o_rg, p["mg"], x, mod, wts["norm2_g"], wts, layer, cfg["bb"], cfg["tb"],
                   rg_time_major=fresh)
    out = _peer(h2, y, mod, wts, final_g, layer, cfg["peer_bb"], cfg["peer_tb"], cfg["peer_ec"], final)
    small_state = (n_new, m_new[:, 0, SM_I:SM_I + N_HEADS], h_fin,
                   jnp.swapaxes(buf_new.reshape(CONV_W - 1, bn, d), 0, 1))
    return out, (gla_stack, mlc_stack), small_state


def kernel(x_prompt, x_sample, state_gla, state_mlstm_c, state_mlstm_n, state_mlstm_m, state_rglru_h, state_rglru_conv, c_prompt, c_sample, w_ada, b_ada, norm1_g, norm2_g, w_in, w_gla_f2, b_gla_f, gla_norm_g, b_ml_i, b_ml_f, ml_norm_g, conv_w, conv_b, w_rg_a, b_rg_a, w_rg_x, b_rg_x, rg_lambda, w_br_gla, w_br_ml, w_br_rg, w_out, w_peer_q, peer_keys, peer_u, peer_v, final_norm_g):
    n_layers = w_in.shape[0]
    bp, tp, d = x_prompt.shape
    bs, ts, _ = x_sample.shape
    wts = _prepare_weights(w_in, w_gla_f2, b_gla_f, gla_norm_g, b_ml_i, b_ml_f, ml_norm_g, conv_w, conv_b, w_rg_a,
                           b_rg_a, w_rg_x, b_rg_x, rg_lambda, w_br_gla, w_br_ml, w_br_rg, w_out, w_peer_q,
                           peer_keys, peer_u, peer_v, norm1_g, norm2_g)
    final_g = final_norm_g.reshape(1, 1, d)
    mod_all = _modulation(jnp.concatenate([c_prompt, c_sample], axis=0), w_ada, b_ada)
    m0_pad = jnp.pad(state_mlstm_m, ((0, 0), (0, 0), (SM_I, SMALL_W - SM_I - N_HEADS))).reshape(n_layers, bs, 1, SMALL_W)
    conv_tm = jnp.swapaxes(state_rglru_conv, 1, 2).reshape(n_layers, (CONV_W - 1) * bs, d)
    sample_state = (state_gla, state_mlstm_c, state_mlstm_n, m0_pad, state_rglru_h, conv_tm)

    cfg_p = dict(bb=1, tb=min(256, tp), gla_bb=min(4, bp), ml_bb=min(2, bp), rg_tc=min(128, tp), peer_bb=1,
                 peer_tb=min(512, tp), peer_ec=1024)
    cfg_s = dict(bb=min(16, bs), tb=ts, gla_bb=min(4, bs), ml_bb=min(4, bs), rg_tc=ts, peer_bb=min(512 // ts, bs),
                 peer_tb=ts, peer_ec=1024)
    yp, ys = x_prompt, x_sample
    new_p, new_s = [], []
    big_p = big_s = None
    for l in range(n_layers):
        mod_p = mod_all[l, :bp].reshape(bp, 1, 6 * d)
        mod_s = mod_all[l, bp:].reshape(bs, 1, 6 * d)
        yp, big_p, st_p = _layer(yp, mod_p, None, wts, final_g, l, n_layers, big_p, cfg_p)
        ys, big_s, st_s = _layer(ys, mod_s, sample_state, wts, final_g, l, n_layers, big_s, cfg_s)
        new_p.append(st_p)
        new_s.append(st_s)
    stacked_p = [jnp.stack(z) for z in zip(*new_p)]
    stacked_s = [jnp.stack(z) for z in zip(*new_s)]
    return (yp, ys, *big_p, *stacked_p, *big_s, *stacked_s)
```

```python
import functools

import jax
import jax.numpy as jnp
from jax import lax
from jax.experimental import pallas as pl
from jax.experimental.pallas import tpu as pltpu

F32 = jnp.float32
BF16 = jnp.bfloat16
HIGHEST = lax.Precision.HIGHEST

EPS = 1e-6
N_HEADS = 4
DK = 128
DV = 256
GLA_RANK = 16
GLA_GATE_TEMP = 16.0
CHUNK = 64
RG_BLOCKS = 8
RG_C = 8.0
CONV_W = 4
PEER_HEADS = 8
PEER_NKEYS = 128
PEER_TOPK = 16
NEG = -1e30
LANES = 128

VMEM_LIMIT_BYTES = 56 * 1024 * 1024
PEER_VMEM_LIMIT_BYTES = 60 * 1024 * 1024

IN_PIECES = (("gq", 512), ("gk", 512), ("gv", 1024), ("gg", 1024),
             ("mq", 512), ("mk", 512), ("mv", 1024), ("mo", 1024),
             ("rx", 1024), ("mg", 3072))
SMALL_W = 128
SM_I = GLA_RANK
SM_F = GLA_RANK + N_HEADS


def _params(sem, vmem_limit_bytes=VMEM_LIMIT_BYTES):
    return pltpu.CompilerParams(dimension_semantics=sem, vmem_limit_bytes=vmem_limit_bytes)


def _const_spec(shape, index_map):
    return pl.BlockSpec(shape, index_map, pipeline_mode=pl.Buffered(1))


def _log_sigmoid(x):
    return jnp.minimum(x, 0.0) - jnp.log1p(jnp.exp(-jnp.abs(x)))


def _softplus(x):
    return jnp.maximum(x, 0.0) + jnp.log1p(jnp.exp(-jnp.abs(x)))


def _sigmoid(x):
    return 1.0 / (1.0 + jnp.exp(-x))


def _rms(x, g):
    return x * lax.rsqrt(jnp.mean(x * x, axis=-1, keepdims=True) + EPS) * g


def _pad_rows(a, rows, value=0.0):
    if a.shape[0] == rows:
        return a
    return jnp.concatenate([a, jnp.full((rows - a.shape[0], a.shape[1]), value, a.dtype)], axis=0)


def _row_to_col(row, eye):
    n = row.shape[1]
    return jnp.sum(jnp.where(eye, jnp.broadcast_to(row, (n, n)), 0.0), axis=1, keepdims=True)


def _col_to_row(col, eye):
    n = col.shape[0]
    return jnp.sum(jnp.where(eye, jnp.broadcast_to(col, (n, n)), 0.0), axis=0, keepdims=True)


def _mod_kernel(c_ref, w_ref, b_ref, o_ref):
    c = c_ref[...]
    s = (c * _sigmoid(c)).astype(BF16)
    o_ref[0] = jnp.dot(s, w_ref[0].astype(BF16), preferred_element_type=F32) + b_ref[0]


def _modulation(c_all, w_ada, b_ada):
    n_layers, d, n6 = w_ada.shape
    rows = c_all.shape[0]
    tn = 1536
    return pl.pallas_call(
        _mod_kernel,
        grid=(n_layers, n6 // tn),
        in_specs=[pl.BlockSpec((rows, d), lambda l, n: (0, 0)),
                  pl.BlockSpec((1, d, tn), lambda l, n: (l, 0, n)),
                  pl.BlockSpec((1, 1, tn), lambda l, n: (l, 0, n))],
        out_specs=pl.BlockSpec((1, rows, tn), lambda l, n: (l, 0, n)),
        out_shape=jax.ShapeDtypeStruct((n_layers, rows, n6), F32),
        compiler_params=_params(("arbitrary", "arbitrary")),
        name="adaln_mod",
    )(c_all, w_ada, b_ada.reshape(n_layers, 1, n6))


def _inproj_kernel(x_ref, mod_ref, g_ref, wm_ref, ws_ref, *out_refs):
    bb, tb, d = x_ref.shape
    mod = mod_ref[...]
    h = _rms(x_ref[...], g_ref[...]) * (1.0 + mod[:, :, d:2 * d]) + mod[:, :, 0:d]
    hb = h.reshape(bb * tb, d).astype(BF16)
    off = 0
    for r, (_, w) in zip(out_refs[:-1], IN_PIECES):
        r[...] = jnp.dot(hb, wm_ref[0, :, off:off + w], preferred_element_type=F32).astype(r.dtype)
        off += w
    out_refs[-1][...] = jnp.dot(hb, ws_ref[0], preferred_element_type=F32)


def _in_projection(x, mod, g1, w_main, w_small, layer, bb, tb, out_dtype, rx_time_major):
    bn, t, d = x.shape
    nt = t // tb
    rows = bb * tb
    n_main = w_main.shape[-1]
    out_shapes, out_specs = [], []
    for name, w in IN_PIECES:
        if name == "rx" and rx_time_major:
            out_shapes.append(jax.ShapeDtypeStruct((t, bn * w), out_dtype))
            out_specs.append(pl.BlockSpec((tb, w), lambda i, j: (j, i)))
        else:
            out_shapes.append(jax.ShapeDtypeStruct((bn * t, w), out_dtype))
            out_specs.append(pl.BlockSpec((rows, w), lambda i, j: (i * nt + j, 0)))
    out_shapes.append(jax.ShapeDtypeStruct((bn * t, SMALL_W), F32))
    out_specs.append(pl.BlockSpec((rows, SMALL_W), lambda i, j: (i * nt + j, 0)))
    if rx_time_major:
        assert bb == 1
    return pl.pallas_call(
        _inproj_kernel,
        grid=(bn // bb, nt),
        in_specs=[pl.BlockSpec((bb, tb, d), lambda i, j: (i, j, 0)),
                  pl.BlockSpec((bb, 1, 6 * d), lambda i, j: (i, 0, 0)),
                  _const_spec((1, 1, d), lambda i, j: (layer, 0, 0)),
                  _const_spec((1, d, n_main), lambda i, j: (layer, 0, 0)),
                  _const_spec((1, d, SMALL_W), lambda i, j: (layer, 0, 0))],
        out_specs=out_specs,
        out_shape=out_shapes,
        compiler_params=_params(("arbitrary", "arbitrary")),
        name="in_proj",
    )(x, mod, g1, w_main, w_small)


def _gla_kernel(*refs, bb, tc, has_state, has_stack):
    if has_stack:
        refs = refs[:-3] + refs[-2:]
    if has_state:
        gq, gk, gv, gg, sm, wf2, bf, gn, s0, o_ref, s_ref = refs
    else:
        gq, gk, gv, gg, sm, wf2, bf, gn, o_ref, s_ref = refs
        s0 = None
    c = CHUNK

    @pl.when(pl.program_id(1) == 0)
    def _():
        if has_state:
            s_ref[...] = s0[...]
        else:
            s_ref[...] = jnp.zeros_like(s_ref)

    ri = lax.broadcasted_iota(jnp.int32, (c, c), 0)
    ci = lax.broadcasted_iota(jnp.int32, (c, c), 1)
    causal = ri >= ci
    tril = causal.astype(F32)
    hk = N_HEADS * DK
    valid = lax.broadcasted_iota(jnp.int32, (c, hk), 0) < tc
    eye = (lax.broadcasted_iota(jnp.int32, (DK, DK), 0) == lax.broadcasted_iota(jnp.int32, (DK, DK), 1))
    nt_dims = (((1,), (1,)), ((), ()))
    tn_dims = (((0,), (0,)), ((), ()))
    for s in range(bb):
        smb = _pad_rows(sm[s], c).astype(BF16)
        x = jnp.dot(smb, wf2[0], preferred_element_type=F32) + bf[0]
        logf = jnp.where(valid, _log_sigmoid(x) * (1.0 / GLA_GATE_TEMP), 0.0)
        b = jnp.dot(tril, logf, precision=HIGHEST, preferred_element_type=F32)
        b_last = b[c - 1:c, :]
        q = _pad_rows(gq[s].astype(F32), c) * (DK ** -0.5)
        k = _pad_rows(gk[s].astype(F32), c)
        qd = (q * jnp.exp(b)).astype(BF16)
        kd = (k * jnp.exp(-b)).astype(BF16)
        kend = (k * jnp.exp(b_last - b)).astype(BF16)
        vb = _pad_rows(gv[s].astype(F32), c).astype(BF16)
        g = gg[s].astype(F32)
        g = g * _sigmoid(g)
        for h in range(N_HEADS):
            kcols = slice(h * DK, (h + 1) * DK)
            vcols = slice(h * DV, (h + 1) * DV)
            att = lax.dot_general(qd[:, kcols], kd[:, kcols], nt_dims, preferred_element_type=F32)
            att = jnp.where(causal, att, 0.0).astype(BF16)
            st = s_ref[s, h]
            o = (jnp.dot(att, vb[:, vcols], preferred_element_type=F32)
                 + jnp.dot(qd[:, kcols], st.astype(BF16), preferred_element_type=F32))
            dec = jnp.exp(_row_to_col(b_last[:, kcols], eye))
            s_ref[s, h] = dec * st + lax.dot_general(kend[:, kcols], vb[:, vcols], tn_dims,
                                                     preferred_element_type=F32)
            o_ref[s, :, vcols] = (_rms(o[:tc], gn[0]) * g[:, vcols]).astype(o_ref.dtype)


def _gla_mixer(p, small, wf2, bfv, gn, s0, layer, n_layers, stack, bn, t, bb, out_dtype):
    tc = min(CHUNK, t)
    nch = t // tc
    assert tc * nch == t
    rspec = lambda w: pl.BlockSpec((bb, tc, w), lambda i, j: (i, j, 0))
    r3 = lambda a: a.reshape(bn, t, a.shape[-1])
    in_specs = [rspec(512), rspec(512), rspec(1024), rspec(1024), rspec(SMALL_W),
                _const_spec((1, SMALL_W, 512), lambda i, j: (layer, 0, 0)),
                _const_spec((1, 1, 512), lambda i, j: (layer, 0, 0)),
                _const_spec((1, 1, DV), lambda i, j: (layer, 0, 0))]
    args = [r3(p["gq"]), r3(p["gk"]), r3(p["gv"]), r3(p["gg"]), r3(small), wf2, bfv, gn]
    sspec = pl.BlockSpec((None, bb, N_HEADS, DK, DV), lambda i, j: (layer, i, 0, 0, 0))
    if s0 is not None:
        in_specs.append(pl.BlockSpec((None, bb, N_HEADS, DK, DV), lambda i, j: (layer, i, 0, 0, 0)))
        args.append(s0)
    aliases = {}
    if stack is not None:
        in_specs.append(pl.BlockSpec(memory_space=pl.ANY))
        args.append(stack)
        aliases = {len(args) - 1: 1}
    o, s_stack = pl.pallas_call(
        functools.partial(_gla_kernel, bb=bb, tc=tc, has_state=s0 is not None, has_stack=stack is not None),
        grid=(bn // bb, nch),
        in_specs=in_specs,
        out_specs=[rspec(1024), sspec],
        out_shape=[jax.ShapeDtypeStruct((bn, t, 1024), out_dtype),
                   jax.ShapeDtypeStruct((n_layers, bn, N_HEADS, DK, DV), F32)],
        input_output_aliases=aliases,
        compiler_params=_params(("arbitrary", "arbitrary")),
        name="gla_mixer",
    )(*args)
    return o.reshape(bn * t, 1024), s_stack


def _mlstm_kernel(*refs, bb, tc, has_state, has_stack):
    if has_stack:
        refs = refs[:-5] + refs[-4:]
    if has_state:
        mq, mk, mv, mo, sm, bias, gn, c0, n0, m0, o_ref, c_ref, n_ref, m_ref = refs
    else:
        mq, mk, mv, mo, sm, bias, gn, o_ref, c_ref, n_ref, m_ref = refs
    c = CHUNK

    @pl.when(pl.program_id(1) == 0)
    def _():
        if has_state:
            c_ref[...] = c0[...]
            n_ref[...] = n0[...]
            m_ref[...] = m0[...]
        else:
            c_ref[...] = jnp.zeros_like(c_ref)
            n_ref[...] = jnp.zeros_like(n_ref)
            m_ref[...] = jnp.zeros_like(m_ref)

    ri = lax.broadcasted_iota(jnp.int32, (c, c), 0)
    ci = lax.broadcasted_iota(jnp.int32, (c, c), 1)
    causal = ri >= ci
    eye = ri == ci
    tril = causal.astype(F32)
    trow = lax.broadcasted_iota(jnp.int32, (c, SMALL_W), 0)
    valid = trow < tc
    nt_dims = (((1,), (1,)), ((), ()))
    tn_dims = (((0,), (0,)), ((), ()))
    for s in range(bb):
        t1 = _pad_rows(sm[s], c) + bias[0]
        ig_t = jnp.where(valid, t1, NEG)
        lf_t = jnp.where(valid, _log_sigmoid(t1), 0.0)
        f_t = jnp.dot(tril, lf_t, precision=HIGHEST, preferred_element_type=F32)
        fa_t = pltpu.roll(f_t, SMALL_W - (SM_F - SM_I), 1)
        a_t = ig_t - fa_t
        cm_t = a_t
        for sh in (1, 2, 4, 8, 16, 32):
            cm_t = jnp.maximum(cm_t, jnp.where(trow >= sh, pltpu.roll(cm_t, sh, 0), NEG))
        m_prev = m_ref[s]
        m_t = fa_t + jnp.maximum(m_prev, cm_t)
        wi_t = jnp.exp(fa_t + m_prev - m_t)
        fm_t = fa_t - m_t
        em_t = jnp.exp(-m_t)
        m_last_row = m_t[c - 1:c, :]
        f_last_row = fa_t[c - 1:c, :]
        ws_t = jnp.exp(ig_t + f_last_row - fa_t - m_last_row)
        carry_row = jnp.exp(f_last_row + m_prev - m_last_row)
        qb_all = _pad_rows(mq[s].astype(F32), c).astype(BF16)
        k_all = _pad_rows(mk[s].astype(F32), c) * (DK ** -0.5)
        vb_all = _pad_rows(mv[s].astype(F32), c).astype(BF16)
        g_all = _sigmoid(mo[s].astype(F32))
        for h in range(N_HEADS):
            kcols = slice(h * DK, (h + 1) * DK)
            vcols = slice(h * DV, (h + 1) * DV)
            qb = qb_all[:, kcols]
            q = qb.astype(F32)
            k = k_all[:, kcols]
            vb = vb_all[:, vcols]
            col = slice(SM_I + h, SM_I + h + 1)
            a_row = _col_to_row(a_t[:, col], eye)
            w_inter = wi_t[:, col]
            dmat = jnp.exp(jnp.where(causal, fm_t[:, col] + a_row, NEG))
            qk = lax.dot_general(qb, k.astype(BF16), nt_dims, preferred_element_type=F32) * dmat
            cs = c_ref[s, h]
            ns = n_ref[s, h:h + 1, :]
            num = (jnp.dot(qk.astype(BF16), vb, preferred_element_type=F32)
                   + w_inter * jnp.dot(qb, cs.astype(BF16), preferred_element_type=F32))
            den = jnp.sum(qk, axis=1, keepdims=True) + w_inter * jnp.sum(q * ns, axis=1, keepdims=True)
            hh = num / jnp.maximum(jnp.abs(den), em_t[:, col])
            carry = carry_row[:, col]
            kw = k * ws_t[:, col]
            c_ref[s, h] = carry * cs + lax.dot_general(kw.astype(BF16), vb, tn_dims, preferred_element_type=F32)
            n_ref[s, h:h + 1, :] = carry * ns + jnp.sum(kw, axis=0, keepdims=True)
            o_ref[s, :, vcols] = (_rms(hh[:tc], gn[0]) * g_all[:, vcols]).astype(o_ref.dtype)
        m_ref[s] = m_last_row


def _mlstm_mixer(p, small, bias, gn, state, layer, n_layers, stack, bn, t, bb, out_dtype):
    tc = min(CHUNK, t)
    nch = t // tc
    assert tc * nch == t
    rspec = lambda w: pl.BlockSpec((bb, tc, w), lambda i, j: (i, j, 0))
    r3 = lambda a: a.reshape(bn, t, a.shape[-1])
    in_specs = [rspec(512), rspec(512), rspec(1024), rspec(1024), rspec(SMALL_W),
                _const_spec((1, 1, SMALL_W), lambda i, j: (layer, 0, 0)),
                _const_spec((1, 1, DV), lambda i, j: (layer, 0, 0))]
    args = [r3(p["mq"]), r3(p["mk"]), r3(p["mv"]), r3(p["mo"]), r3(small), bias, gn]
    cspec = pl.BlockSpec((None, bb, N_HEADS, DK, DV), lambda i, j: (layer, i, 0, 0, 0))
    nspec = pl.BlockSpec((bb, N_HEADS, DK), lambda i, j: (i, 0, 0))
    mspec = pl.BlockSpec((bb, 1, SMALL_W), lambda i, j: (i, 0, 0))
    if state is not None:
        c0, n0, m0 = state
        in_specs += [pl.BlockSpec((None, bb, N_HEADS, DK, DV), lambda i, j: (layer, i, 0, 0, 0)),
                     pl.BlockSpec((None, bb, N_HEADS, DK), lambda i, j: (layer, i, 0, 0)),
                     pl.BlockSpec((None, bb, 1, SMALL_W), lambda i, j: (layer, i, 0, 0))]
        args += [c0, n0, m0]
    aliases = {}
    if stack is not None:
        in_specs.append(pl.BlockSpec(memory_space=pl.ANY))
        args.append(stack)
        aliases = {len(args) - 1: 1}
    o, c_stack, n_new, m_new = pl.pallas_call(
        functools.partial(_mlstm_kernel, bb=bb, tc=tc, has_state=state is not None, has_stack=stack is not None),
        grid=(bn // bb, nch),
        in_specs=in_specs,
        out_specs=[rspec(1024), cspec, nspec, mspec],
        out_shape=[jax.ShapeDtypeStruct((bn, t, 1024), out_dtype),
                   jax.ShapeDtypeStruct((n_layers, bn, N_HEADS, DK, DV), F32),
                   jax.ShapeDtypeStruct((bn, N_HEADS, DK), F32),
                   jax.ShapeDtypeStruct((bn, 1, SMALL_W), F32)],
        input_output_aliases=aliases,
        compiler_params=_params(("arbitrary", "arbitrary")),
        name="mlstm_mixer",
    )(*args)
    return o.reshape(bn * t, 1024), c_stack, n_new, m_new


def _rglru_kernel(*refs, nb, tc, has_state):
    if has_state:
        rx, cb, h0, cw, cbias, wa, ba, wx, bx, lam, hs_ref, hf_ref, nb_ref, xpad, a_s, u_s = refs
    else:
        rx, cw, cbias, wa, ba, wx, bx, lam, hs_ref, hf_ref, nb_ref, xpad, a_s, u_s = refs
    hist = (CONV_W - 1) * nb
    rows = tc * nb
    w = rx.shape[1]
    bw = w // RG_BLOCKS

    @pl.when(pl.program_id(0) == 0)
    def _():
        if has_state:
            xpad[0:hist, :] = cb[0]
            hf_ref[...] = h0[0]
        else:
            xpad[0:hist, :] = jnp.zeros((hist, w), F32)
            hf_ref[...] = jnp.zeros_like(hf_ref)

    xpad[hist:hist + rows, :] = rx[...].astype(F32)
    xc = cbias[0]
    for j in range(CONV_W):
        xc = xc + xpad[j * nb:j * nb + rows, :] * cw[0, j:j + 1, :]
    sp = _softplus(-lam[0])
    for n in range(RG_BLOCKS):
        cols = slice(n * bw, (n + 1) * bw)
        xn = xc[:, cols]
        xg = xn.astype(BF16)
        r = _sigmoid(jnp.dot(xg, wa[0, n], preferred_element_type=F32) + ba[0, :, cols])
        i = _sigmoid(jnp.dot(xg, wx[0, n], preferred_element_type=F32) + bx[0, :, cols])
        log_a = -RG_C * r * sp[:, cols]
        a_s[:, cols] = jnp.exp(log_a)
        u_s[:, cols] = jnp.sqrt(1.0 - jnp.exp(2.0 * log_a)) * (i * xn)

    def body(t, h):
        o = pl.multiple_of(t * nb, nb)
        h = a_s[pl.ds(o, nb), :] * h + u_s[pl.ds(o, nb), :]
        hs_ref[pl.ds(o, nb), :] = h.astype(hs_ref.dtype)
        return h

    hf_ref[...] = lax.fori_loop(0, tc, body, hf_ref[...])
    tail = xpad[rows:rows + hist, :]
    xpad[0:hist, :] = tail
    nb_ref[...] = tail


def _rglru(rx, conv_buf, h0, wts, layer, nb, t, tc, out_dtype):
    w = rx.shape[1]
    rows = tc * nb
    hist = (CONV_W - 1) * nb
    bw = w // RG_BLOCKS
    lsp = lambda shape: _const_spec((1,) + shape, lambda j: (layer,) + (0,) * len(shape))
    in_specs = [pl.BlockSpec((rows, w), lambda j: (j, 0))]
    args = [rx]
    if conv_buf is not None:
        in_specs += [lsp((hist, w)), lsp((nb, w))]
        args += [conv_buf, h0]
    in_specs += [lsp((CONV_W, w)), lsp((1, w)), lsp((RG_BLOCKS, bw, bw)), lsp((1, w)),
                 lsp((RG_BLOCKS, bw, bw)), lsp((1, w)), lsp((1, w))]
    args += [wts["conv_w"], wts["conv_b"], wts["w_rg_a"], wts["b_rg_a"], wts["w_rg_x"], wts["b_rg_x"],
             wts["rg_lambda"]]
    return pl.pallas_call(
        functools.partial(_rglru_kernel, nb=nb, tc=tc, has_state=conv_buf is not None),
        grid=(t // tc,),
        in_specs=in_specs,
        out_specs=[pl.BlockSpec((rows, w), lambda j: (j, 0)),
                   pl.BlockSpec((nb, w), lambda j: (0, 0)),
                   pl.BlockSpec((hist, w), lambda j: (0, 0))],
        out_shape=[jax.ShapeDtypeStruct((t * nb, w), out_dtype),
                   jax.ShapeDtypeStruct((nb, w), F32),
                   jax.ShapeDtypeStruct((hist, w), F32)],
        scratch_shapes=[pltpu.VMEM((rows + hist, w), F32), pltpu.VMEM((rows, w), F32), pltpu.VMEM((rows, w), F32)],
        compiler_params=_params(("arbitrary",)),
        name="rglru",
    )(*args)


def _merge_kernel(og, om, orr, mg, x_ref, mod_ref, g2, wbg, wbm, wbr, wo, y_ref, h2_ref):
    bb, tb, d = x_ref.shape
    gates = mg[...].astype(F32)
    merged = (_sigmoid(gates[:, 0:d]) * jnp.dot(og[...].astype(BF16), wbg[0], preferred_element_type=F32)
              + _sigmoid(gates[:, d:2 * d]) * jnp.dot(om[...].astype(BF16), wbm[0], preferred_element_type=F32)
              + _sigmoid(gates[:, 2 * d:3 * d]) * jnp.dot(orr[...].astype(BF16), wbr[0], preferred_element_type=F32))
    z = jnp.dot(merged.astype(BF16), wo[0], preferred_element_type=F32)
    mod = mod_ref[...]
    y = x_ref[...] + mod[:, :, 2 * d:3 * d] * z.reshape(bb, tb, d)
    y_ref[...] = y
    h2 = _rms(y, g2[...]) * (1.0 + mod[:, :, 4 * d:5 * d]) + mod[:, :, 3 * d:4 * d]
    h2_ref[...] = h2.reshape(bb * tb, d).astype(BF16)


def _merge(og, om, orr, mg, x, mod, g2, wts, layer, bb, tb, rg_time_major):
    bn, t, d = x.shape
    nt = t // tb
    rows = bb * tb
    rspec = lambda w: pl.BlockSpec((rows, w), lambda i, j: (i * nt + j, 0))
    if rg_time_major:
        assert bb == 1
        rg_spec = pl.BlockSpec((tb, d), lambda i, j: (j, i))
    else:
        rg_spec = rspec(d)
    wspec = _const_spec((1, d, d), lambda i, j: (layer, 0, 0))
    return pl.pallas_call(
        _merge_kernel,
        grid=(bn // bb, nt),
        in_specs=[rspec(d), rspec(d), rg_spec, rspec(3 * d),
                  pl.BlockSpec((bb, tb, d), lambda i, j: (i, j, 0)),
                  pl.BlockSpec((bb, 1, 6 * d), lambda i, j: (i, 0, 0)),
                  _const_spec((1, 1, d), lambda i, j: (layer, 0, 0)),
                  wspec, wspec, wspec, wspec],
        out_specs=[pl.BlockSpec((bb, tb, d), lambda i, j: (i, j, 0)), rspec(d)],
        out_shape=[jax.ShapeDtypeStruct((bn, t, d), F32), jax.ShapeDtypeStruct((bn * t, d), BF16)],
        compiler_params=_params(("arbitrary", "arbitrary")),
        name="merge",
    )(og, om, orr, mg, x, mod, g2, wts["w_br_gla"], wts["w_br_ml"], wts["w_br_rg"], wts["w_out"])


def _sort16_pairs():
    n, pairs, p = 16, [], 1
    while p < n:
        k = p
        while k >= 1:
            for j in range(k % p, n - k, 2 * k):
                for i in range(min(k, n - j - k)):
                    if (i + j) // (2 * p) == (i + j + k) // (2 * p):
                        pairs.append((i + j, i + j + k))
            k //= 2
        p *= 2
    return pairs


_SORT16 = _sort16_pairs()
_BITONIC16 = [(i, i + dd) for dd in (8, 4, 2, 1) for i in range(16) if (i & dd) == 0]
_CANDS = [(a, b) for a in range(PEER_TOPK) for b in range(PEER_TOPK) if (a + 1) * (b + 1) <= PEER_TOPK]


def _top16_sorted(st):
    xs = [st[g * 8:(g + 1) * 8, :] for g in range(16)]
    for i, j in _SORT16:
        hi, lo = jnp.maximum(xs[i], xs[j]), jnp.minimum(xs[i], xs[j])
        xs[i], xs[j] = hi, lo
    for shift in (4, 2, 1):
        ys = [jnp.maximum(xs[r], pltpu.roll(xs[15 - r], shift, 0)) for r in range(16)]
        for i, j in _BITONIC16:
            hi, lo = jnp.maximum(ys[i], ys[j]), jnp.minimum(ys[i], ys[j])
            ys[i], ys[j] = hi, lo
        xs = ys
    return xs


def _dup_bf16_words(x):
    bits = lax.bitcast_convert_type(x.astype(BF16).astype(F32), jnp.uint32)
    return bits | (bits >> 16)


def _peer_kernel(h2_ref, y_ref, mod_ref, wq_ref, keys_ref, u_ref, vt_ref, fg_ref, o_ref,
                 r1b_ref, e1b_ref, n0w_ref, ccw_ref, pta_ref, ptb_ref, gt_ref, acc_ref, h2t_ref,
                 *, final, sub):
    bb, tb, d = y_ref.shape
    tm = bb * tb
    ec = u_ref.shape[1]
    cidx = pl.program_id(1)
    nt_dims = (((1,), (1,)), ((), ()))
    ngrp = PEER_NKEYS // 8
    assert ec >= 2 * PEER_HEADS * PEER_NKEYS and ec >= PEER_HEADS * PEER_TOPK * 8
    band = lambda p, h: slice((p * PEER_HEADS + h) * PEER_NKEYS, (p * PEER_HEADS + h + 1) * PEER_NKEYS)
    top8 = lambda h, r: slice((h * PEER_TOPK + r) * 8, (h * PEER_TOPK + r + 1) * 8)

    @pl.when(cidx == 0)
    def _():
        q = jnp.dot(h2_ref[...], wq_ref[0], preferred_element_type=F32)
        for p in range(2):
            for h in range(PEER_HEADS):
                col = (h * 2 + p) * PEER_NKEYS
                qs = q[:, col:col + PEER_NKEYS].astype(BF16)
                pta_ref[band(p, h), :] = lax.dot_general(keys_ref[0, h, p], qs, nt_dims,
                                                         preferred_element_type=F32)
        sub8 = lax.broadcasted_iota(jnp.int32, (8, LANES), 0)
        rep = lambda row: jnp.broadcast_to(row, (8, LANES))[None]
        for l in range(tm // LANES):
            lanes = slice(l * LANES, (l + 1) * LANES)
            tops = []
            for p in range(2):
                stack = [jnp.zeros((8, LANES), F32) for _ in range(PEER_TOPK)]
                for h in range(PEER_HEADS):
                    st = pta_ref[band(p, h), lanes]
                    srt = _top16_sorted(st)
                    if p == 1:
                        st3 = st.reshape(ngrp, 8, LANES)
                        rank = jnp.full((ngrp, 8, LANES), float(PEER_TOPK), F32)
                        for r in reversed(range(PEER_TOPK)):
                            ptb_ref[top8(h, r), lanes] = srt[r]
                            rank = jnp.where(srt[r][None] > st3, rank, float(r))
                        r1b_ref[h, :, lanes] = rank.reshape(PEER_NKEYS, LANES).astype(BF16)
                        e1b_ref[h, :, lanes] = jnp.exp(st3 - srt[0][None]).reshape(PEER_NKEYS, LANES).astype(BF16)
                    stack = [jnp.where(sub8 == h, srt[r], stack[r]) for r in range(PEER_TOPK)]
                tops.append(stack)
            work = [tops[0][a] + tops[1][b] for a, b in _CANDS]
            theta = jnp.full((8, LANES), NEG, F32)
            cnt = jnp.zeros((8, LANES), F32)
            for _ in range(PEER_TOPK):
                mx = functools.reduce(jnp.maximum, work)
                eq = [w == mx for w in work]
                theta = jnp.where(cnt < PEER_TOPK, mx, theta)
                cnt = cnt + functools.reduce(jnp.add, [e.astype(F32) for e in eq])
                work = [jnp.where(e, NEG, w) for e, w in zip(eq, work)]
            top = tops[0][0] + tops[1][0]
            z = jnp.zeros((8, LANES), F32)
            for a, b in _CANDS:
                cd = tops[0][a] + tops[1][b]
                z = z + jnp.where(cd >= theta, jnp.exp(cd - top), 0.0)
            inv_z = 1.0 / z
            for h in range(PEER_HEADS):
                s03 = pta_ref[band(0, h), lanes].reshape(ngrp, 8, LANES)
                th8 = rep(theta[h:h + 1, :])
                n0 = jnp.full((ngrp, 8, LANES), float(PEER_TOPK), F32)
                for b in reversed(range(PEER_TOPK)):
                    n0 = jnp.where(s03 + ptb_ref[top8(h, b), lanes][None] >= th8, n0, float(b))
                cc = jnp.exp(s03 - rep(tops[0][0][h:h + 1, :])) * rep(inv_z[h:h + 1, :])
                n0w_ref[h, l, 0:PEER_NKEYS, :] = _dup_bf16_words(n0.reshape(PEER_NKEYS, LANES))
                ccw_ref[h, l, 0:PEER_NKEYS, :] = _dup_bf16_words(cc.reshape(PEER_NKEYS, LANES))
        acc_ref[...] = jnp.zeros_like(acc_ref)
        ptb_ref[...] = jnp.zeros_like(ptb_ref)
        h2t_ref[...] = h2_ref[...].astype(F32).T.astype(BF16)

    groups = ec // PEER_NKEYS
    nsub = PEER_NKEYS // sub
    prev = jnp.maximum(cidx - 1, 0)

    def step(pt_next, pt_cur):
        pt_next[...] = jnp.dot(u_ref[0], h2t_ref[...], preferred_element_type=F32)

        def row(ref, h, i):
            return jnp.concatenate([pltpu.bitcast(ref[h, l, pl.ds(i, sub // 2, stride=0), :], BF16)
                                    for l in range(tm // LANES)], axis=1)

        for g in range(groups):
            i = prev * groups + g
            for jb in range(nsub):
                j0 = jb * sub
                r0 = g * PEER_NKEYS + j0
                pre = pt_cur[r0:r0 + sub, :]
                act = (0.5 * pre * (1.0 + lax.erf(pre * 0.7071067811865476))).astype(BF16)
                wgt = jnp.zeros((sub, tm), BF16)
                for h in range(PEER_HEADS):
                    val = e1b_ref[h, j0:j0 + sub, :] * row(ccw_ref, h, i)
                    wgt = wgt + jnp.where(r1b_ref[h, j0:j0 + sub, :] < row(n0w_ref, h, i), val, jnp.zeros_like(val))
                gt_ref[r0:r0 + sub, :] = act * wgt
        acc_ref[...] += jnp.dot(vt_ref[0], gt_ref[...], preferred_element_type=F32)

    @pl.when(cidx % 2 == 0)
    def _():
        step(pta_ref, ptb_ref)

    @pl.when(cidx % 2 == 1)
    def _():
        step(ptb_ref, pta_ref)

    @pl.when(cidx == pl.num_programs(1) - 1)
    def _():
        peer = acc_ref[...].T.reshape(bb, tb, d)
        y = y_ref[...] + mod_ref[...][:, :, 5 * d:6 * d] * peer
        if final:
            y = _rms(y, fg_ref[...])
        o_ref[...] = y


def _peer(h2, y, mod, wts, final_g, layer, bb, tb, ec, final):
    bn, t, d = y.shape
    nt = t // tb
    tm = bb * tb
    n_exp = wts["peer_u"].shape[1]
    nq = wts["w_peer_q"].shape[-1]
    sub = 32
    assert tm % LANES == 0
    tile = lambda dt: pltpu.VMEM((PEER_HEADS, PEER_NKEYS, tm), dt)
    row_tile = lambda: pltpu.VMEM((PEER_HEADS, tm // LANES, PEER_NKEYS + 8, LANES), jnp.uint32)
    nc = n_exp // ec
    return pl.pallas_call(
        functools.partial(_peer_kernel, final=final, sub=sub),
        grid=(bn * t // tm, nc + 1),
        in_specs=[_const_spec((tm, d), lambda i, c: (i, 0)),
                  _const_spec((bb, tb, d), lambda i, c: (i // nt, i % nt, 0)),
                  _const_spec((bb, 1, 6 * d), lambda i, c: (i // nt, 0, 0)),
                  _const_spec((1, d, nq), lambda i, c: (layer, 0, 0)),
                  _const_spec((1, PEER_HEADS, 2, PEER_NKEYS, PEER_NKEYS), lambda i, c: (layer, 0, 0, 0, 0)),
                  pl.BlockSpec((1, ec, d), lambda i, c: (layer, jnp.minimum(c, nc - 1), 0)),
                  pl.BlockSpec((1, d, ec), lambda i, c: (layer, 0, jnp.maximum(c - 1, 0))),
                  _const_spec((1, 1, d), lambda i, c: (0, 0, 0))],
        out_specs=pl.BlockSpec((bb, tb, d), lambda i, c: (i // nt, i % nt, 0)),
        out_shape=jax.ShapeDtypeStruct((bn, t, d), F32),
        scratch_shapes=[tile(BF16), tile(BF16), row_tile(), row_tile(),
                        pltpu.VMEM((ec, tm), F32),
                        pltpu.VMEM((ec, tm), F32),
                        pltpu.VMEM((ec, tm), BF16),
                        pltpu.VMEM((d, tm), F32),
                        pltpu.VMEM((d, tm), BF16)],
        compiler_params=_params(("arbitrary", "arbitrary"), PEER_VMEM_LIMIT_BYTES),
        name="peer",
    )(h2, y, mod, wts["w_peer_q"], wts["peer_keys"], wts["peer_u"], wts["peer_vt"], final_g)


def _prepare_weights(w_in, w_gla_f2, b_gla_f, gla_norm_g, b_ml_i, b_ml_f, ml_norm_g, conv_w, conv_b, w_rg_a, b_rg_a,
                     w_rg_x, b_rg_x, rg_lambda, w_br_gla, w_br_ml, w_br_rg, w_out, w_peer_q, peer_keys, peer_u, peer_v,
                     norm1_g, norm2_g):
    n_layers, d, _ = w_in.shape
    offs, o = {}, 0
    for name, w in (("gq", 512), ("gk", 512), ("gv", 1024), ("gg", 1024), ("gf", GLA_RANK),
                    ("mq", 512), ("mk", 512), ("mv", 1024), ("mo", 1024), ("mi", N_HEADS), ("mf", N_HEADS),
                    ("rx", 1024), ("mg", 3072)):
        offs[name] = (o, o + w)
        o += w
    assert o == w_in.shape[-1]
    cut = lambda name: w_in[:, :, offs[name][0]:offs[name][1]]
    w_main = jnp.concatenate([cut(n) for n, _ in IN_PIECES], axis=-1).astype(BF16)
    n_small = GLA_RANK + 2 * N_HEADS
    w_small = jnp.concatenate([cut("gf"), cut("mi"), cut("mf"),
                               jnp.zeros((n_layers, d, SMALL_W - n_small), F32)], axis=-1).astype(BF16)
    wf2 = jnp.concatenate([w_gla_f2, jnp.zeros((n_layers, SMALL_W - GLA_RANK, w_gla_f2.shape[-1]), F32)],
                          axis=1).astype(BF16)
    bias_small = jnp.concatenate([jnp.zeros((n_layers, GLA_RANK), F32), b_ml_i, b_ml_f,
                                  jnp.zeros((n_layers, SMALL_W - n_small), F32)], axis=-1)
    row = lambda a: a.reshape(n_layers, 1, a.shape[-1])
    return dict(
        w_main=w_main, w_small=w_small, wf2=wf2, b_gla_f=row(b_gla_f), gla_norm_g=row(gla_norm_g),
        bias_small=row(bias_small), ml_norm_g=row(ml_norm_g),
        conv_w=conv_w, conv_b=row(conv_b), w_rg_a=w_rg_a.astype(BF16), b_rg_a=row(b_rg_a),
        w_rg_x=w_rg_x.astype(BF16), b_rg_x=row(b_rg_x), rg_lambda=row(rg_lambda),
        w_br_gla=w_br_gla.astype(BF16), w_br_ml=w_br_ml.astype(BF16), w_br_rg=w_br_rg.astype(BF16),
        w_out=w_out.astype(BF16), w_peer_q=w_peer_q.astype(BF16), peer_keys=peer_keys.astype(BF16),
        peer_u=peer_u.astype(BF16), peer_vt=jnp.swapaxes(peer_v, 1, 2).astype(BF16),
        norm1_g=row(norm1_g), norm2_g=row(norm2_g))


def _layer(x, mod, state, wts, final_g, layer, n_layers, stacks, cfg):
    bn, t, d = x.shape
    final = layer == n_layers - 1
    gla_stack, mlc_stack = stacks if stacks is not None else (None, None)
    fresh = state is None
    act_dtype = BF16 if fresh else F32
    outs = _in_projection(x, mod, wts["norm1_g"], wts["w_main"], wts["w_small"], layer,
                          cfg["bb"], cfg["tb"], act_dtype, rx_time_major=fresh)
    p = {name: o for (name, _), o in zip(IN_PIECES, outs[:-1])}
    small = outs[-1]
    if fresh:
        s_gla = ml_state = None
        rx = p["rx"].reshape(t * bn, d)
        conv_buf = h0 = None
    else:
        s_gla, c0, n0, m0, h0, conv_buf = state
        ml_state = (c0, n0, m0)
        rx = jnp.swapaxes(p["rx"].reshape(bn, t, d), 0, 1).reshape(t * bn, d)
    o_gla, gla_stack = _gla_mixer(p, small, wts["wf2"], wts["b_gla_f"], wts["gla_norm_g"], s_gla, layer,
                                  n_layers, gla_stack, bn, t, cfg["gla_bb"], act_dtype)
    o_ml, mlc_stack, n_new, m_new = _mlstm_mixer(p, small, wts["bias_small"], wts["ml_norm_g"], ml_state, layer,
                                                 n_layers, mlc_stack, bn, t, cfg["ml_bb"], act_dtype)
    hs, h_fin, buf_new = _rglru(rx, conv_buf, h0, wts, layer, bn, t, cfg["rg_tc"], act_dtype)
    if fresh:
        o_rg = hs.reshape(t, bn * d)
    else:
        o_rg = jnp.swapaxes(hs.reshape(t, bn, d), 0, 1).reshape(bn * t, d)
    y, h2 = _merge(o_gla, o_ml, o_rg, p["mg"], x, mod, wts["norm2_g"], wts, layer, cfg["bb"], cfg["tb"],
                   rg_time_major=fresh)
    out = _peer(h2, y, mod, wts, final_g, layer, cfg["peer_bb"], cfg["peer_tb"], cfg["peer_ec"], final)
    small_state = (n_new, m_new[:, 0, SM_I:SM_I + N_HEADS], h_fin,
                   jnp.swapaxes(buf_new.reshape(CONV_W - 1, bn, d), 0, 1))
    return out, (gla_stack, mlc_stack), small_state


def kernel(x_prompt, x_sample, state_gla, state_mlstm_c, state_mlstm_n, state_mlstm_m, state_rglru_h, state_rglru_conv, c_prompt, c_sample, w_ada, b_ada, norm1_g, norm2_g, w_in, w_gla_f2, b_gla_f, gla_norm_g, b_ml_i, b_ml_f, ml_norm_g, conv_w, conv_b, w_rg_a, b_rg_a, w_rg_x, b_rg_x, rg_lambda, w_br_gla, w_br_ml, w_br_rg, w_out, w_peer_q, peer_keys, peer_u, peer_v, final_norm_g):
    n_layers = w_in.shape[0]
    bp, tp, d = x_prompt.shape
    bs, ts, _ = x_sample.shape
    wts = _prepare_weights(w_in, w_gla_f2, b_gla_f, gla_norm_g, b_ml_i, b_ml_f, ml_norm_g, conv_w, conv_b, w_rg_a,
                           b_rg_a, w_rg_x, b_rg_x, rg_lambda, w_br_gla, w_br_ml, w_br_rg, w_out, w_peer_q,
                           peer_keys, peer_u, peer_v, norm1_g, norm2_g)
    final_g = final_norm_g.reshape(1, 1, d)
    mod_all = _modulation(jnp.concatenate([c_prompt, c_sample], axis=0), w_ada, b_ada)
    m0_pad = jnp.pad(state_mlstm_m, ((0, 0), (0, 0), (SM_I, SMALL_W - SM_I - N_HEADS))).reshape(n_layers, bs, 1, SMALL_W)
    conv_tm = jnp.swapaxes(state_rglru_conv, 1, 2).reshape(n_layers, (CONV_W - 1) * bs, d)
    sample_state = (state_gla, state_mlstm_c, state_mlstm_n, m0_pad, state_rglru_h, conv_tm)

    cfg_p = dict(bb=1, tb=min(256, tp), gla_bb=min(4, bp), ml_bb=min(2, bp), rg_tc=min(128, tp), peer_bb=1,
                 peer_tb=min(512, tp), peer_ec=2048)
    cfg_s = dict(bb=min(16, bs), tb=ts, gla_bb=min(4, bs), ml_bb=min(4, bs), rg_tc=ts, peer_bb=min(512 // ts, bs),
                 peer_tb=ts, peer_ec=2048)
    yp, ys = x_prompt, x_sample
    new_p, new_s = [], []
    big_p = big_s = None
    for l in range(n_layers):
        mod_p = mod_all[l, :bp].reshape(bp, 1, 6 * d)
        mod_s = mod_all[l, bp:].reshape(bs, 1, 6 * d)
        yp, big_p, st_p = _layer(yp, mod_p, None, wts, final_g, l, n_layers, big_p, cfg_p)
        ys, big_s, st_s = _layer(ys, mod_s, sample_state, wts, final_g, l, n_layers, big_s, cfg_s)
        new_p.append(st_p)
        new_s.append(st_s)
    stacked_p = [jnp.stack(z) for z in zip(*new_p)]
    stacked_s = [jnp.stack(z) for z in zip(*new_s)]
    return (yp, ys, *big_p, *stacked_p, *big_s, *stacked_s)
```
